```python
import math
import jax, jax.numpy as jnp
from jax import lax
import numpy as np

D_MODEL = 1024
BATCH = 8
SEQ = 4096
DEPTH = 2

N_META = 16
D_SSD = D_MODEL
SSD_HEAD_DIM = 64
SSD_HEADS = D_SSD // SSD_HEAD_DIM
SSD_GROUPS = 4
SSD_STATE = 128
SSD_CONV_K = 4
SSD_CHUNK = 128
SSD_CONV_DIM = D_SSD + 2 * SSD_GROUPS * SSD_STATE
GLA_HEADS = 4
GLA_DK = D_MODEL // 2
GLA_DV = D_MODEL
GLA_HEAD_K = GLA_DK // GLA_HEADS
GLA_HEAD_V = GLA_DV // GLA_HEADS
GLA_GATE_RANK = 16
GLA_GATE_TEMP = 16.0
GLA_CHUNK = 64
N_BRANCH = 2
D_BRANCH = D_MODEL
IN_WIDTHS = (
    D_SSD,
    SSD_CONV_DIM,
    SSD_HEADS,
    GLA_DK,
    GLA_DK,
    GLA_DV,
    GLA_DV,
    GLA_GATE_RANK,
    N_BRANCH * D_MODEL,
)
N_IN = sum(IN_WIDTHS)
EPS = 1e-6

kernel_name = "hybrid_ssd_gla_gated_merge"


def rmsnorm(x, w):
    xf = x.astype(jnp.float32)
    y = xf * lax.rsqrt(jnp.mean(xf * xf, axis=-1, keepdims=True) + EPS)
    return (y * w).astype(x.dtype)


def split_cols(t, widths):
    out, start = [], 0
    for w in widths:
        out.append(t[..., start:start + w])
        start += w
    return out


def causal_depthwise_conv(x, w, b):
    k = w.shape[0]
    out = lax.conv_general_dilated(
        x, w[:, None, :].astype(x.dtype), window_strides=(1,), padding=[(k - 1, 0)],
        dimension_numbers=('NWC', 'WIO', 'NWC'), feature_group_count=x.shape[-1])
    return out + b


def ssd_chunked(x, dt, a, bmat, cmat, chunk, state0):
    b, l, h, p = x.shape
    n = bmat.shape[-1]
    nc = l // chunk
    xc = x.reshape(b, nc, chunk, h, p)
    bc = bmat.reshape(b, nc, chunk, h, n)
    cc = cmat.reshape(b, nc, chunk, h, n)
    dtc = dt.reshape(b, nc, chunk, h)
    da = (dt * a.astype(jnp.float32)).reshape(b, nc, chunk, h).transpose(0, 3, 1, 2)
    da_cum = jnp.cumsum(da, axis=-1)
    causal = jnp.tril(jnp.ones((chunk, chunk), dtype=bool))
    seg = da_cum[..., :, None] - da_cum[..., None, :]
    decay = jnp.exp(jnp.where(causal, seg, -jnp.inf))
    scores = jnp.einsum('bzthn,bzshn->bhzts', cc, bc)
    y_diag = jnp.einsum('bhzts,bzsh,bzshp->bzthp', scores * decay, dtc, xc)
    decay_to_end = jnp.exp(da_cum[..., -1:] - da_cum)
    states = jnp.einsum('bzshn,bhzs,bzsh,bzshp->bzhpn', bc, decay_to_end, dtc, xc)
    chunk_decay = jnp.exp(da_cum[..., -1])

    def step(s, inp):
        st, dec = inp
        return dec[..., None, None] * s + st, s

    final, prev = lax.scan(step, state0,
                           (states.transpose(1, 0, 2, 3, 4), chunk_decay.transpose(2, 0, 1)))
    prev = prev.transpose(1, 0, 2, 3, 4)
    y_off = jnp.einsum('bzthn,bzhpn,bhzt->bzthp', cc, prev, jnp.exp(da_cum))
    return (y_diag + y_off).reshape(b, l, h, p), final


def gla_chunked(q, k, v, log_a, chunk, state0):
    b, l, h, dk = q.shape
    nc = l // chunk

    def to_chunks(t):
        return t.reshape(b, nc, chunk, h, t.shape[-1]).transpose(1, 0, 3, 2, 4)

    qc, kc, vc = to_chunks(q), to_chunks(k), to_chunks(v)
    gc = jnp.cumsum(to_chunks(log_a).astype(jnp.float32), axis=3)
    causal = jnp.tril(jnp.ones((chunk, chunk), dtype=bool))[:, :, None]

    def step(s, inp):
        qi, ki, vi, gi = inp
        o_inter = jnp.einsum('bhtk,bhkv->bhtv', qi * jnp.exp(gi), s)
        rel = gi[:, :, :, None, :] - gi[:, :, None, :, :]
        rel = jnp.exp(jnp.where(causal, rel, -jnp.inf))
        att = jnp.einsum('bhtk,bhsk,bhtsk->bhts', qi, ki, rel)
        o_intra = jnp.einsum('bhts,bhsv->bhtv', att, vi)
        g_last = gi[:, :, -1]
        k_dec = ki * jnp.exp(g_last[:, :, None, :] - gi)
        s_new = jnp.exp(g_last)[..., None] * s + jnp.einsum('bhsk,bhsv->bhkv', k_dec, vi)
        return s_new, o_inter + o_intra

    final, o = lax.scan(step, state0, (qc, kc, vc, gc))
    return o.transpose(1, 0, 3, 2, 4).reshape(b, l, h, vc.shape[-1]), final


def hybrid_layer(hres, norm_w, w_in, conv_w, conv_b, dt_bias, a_log, d_skip, ssd_norm_w,
                 gla_gate_w2, gla_gate_b, gla_norm_w, w_branch, w_out):
    b, l, _ = hres.shape
    m = N_META
    u = rmsnorm(hres, norm_w)
    proj = jnp.einsum('bld,dn->bln', u, w_in)
    z, xbc, dt_raw, q, k, v, g_gla, a_lr, merge_g = split_cols(proj, IN_WIDTHS)

    xbc = jax.nn.silu(causal_depthwise_conv(xbc, conv_w, conv_b))
    xs, bm, cm = split_cols(xbc, (D_SSD, SSD_GROUPS * SSD_STATE, SSD_GROUPS * SSD_STATE))
    xs = xs.reshape(b, l, SSD_HEADS, SSD_HEAD_DIM)
    rep = SSD_HEADS // SSD_GROUPS
    bm = jnp.repeat(bm.reshape(b, l, SSD_GROUPS, SSD_STATE), rep, axis=2)
    cm = jnp.repeat(cm.reshape(b, l, SSD_GROUPS, SSD_STATE), rep, axis=2)
    dt = jax.nn.softplus((dt_raw + dt_bias).astype(jnp.float32))
    a = -jnp.exp(a_log.astype(jnp.float32))
    s0 = jnp.zeros((b, SSD_HEADS, SSD_HEAD_DIM, SSD_STATE), jnp.float32)
    y_m, s_m = ssd_chunked(xs[:, :m], dt[:, :m], a, bm[:, :m], cm[:, :m], N_META, s0)
    y_r, _ = ssd_chunked(xs[:, m:], dt[:, m:], a, bm[:, m:], cm[:, m:], SSD_CHUNK, s_m)
    y = jnp.concatenate([y_m, y_r], axis=1) + d_skip[:, None] * xs
    y = y.reshape(b, l, D_SSD) * jax.nn.silu(z)
    y_ssd = rmsnorm(y.reshape(b, l, SSD_GROUPS, D_SSD // SSD_GROUPS),
                    ssd_norm_w.reshape(SSD_GROUPS, -1)).reshape(b, l, D_SSD)

    q = q.reshape(b, l, GLA_HEADS, GLA_HEAD_K) * (GLA_HEAD_K ** -0.5)
    k = k.reshape(b, l, GLA_HEADS, GLA_HEAD_K)
    v = v.reshape(b, l, GLA_HEADS, GLA_HEAD_V)
    gate_logit = jnp.einsum('blr,rk->blk', a_lr, gla_gate_w2) + gla_gate_b
    log_a = (jax.nn.log_sigmoid(gate_logit.astype(jnp.float32)) / GLA_GATE_TEMP)
    log_a = log_a.reshape(b, l, GLA_HEADS, GLA_HEAD_K)
    st0 = jnp.zeros((b, GLA_HEADS, GLA_HEAD_K, GLA_HEAD_V), jnp.float32)
    o_m, st_m = gla_chunked(q[:, :m], k[:, :m], v[:, :m], log_a[:, :m], N_META, st0)
    o_r, _ = gla_chunked(q[:, m:], k[:, m:], v[:, m:], log_a[:, m:], GLA_CHUNK, st_m)
    o = rmsnorm(jnp.concatenate([o_m, o_r], axis=1), gla_norm_w)
    y_gla = o.reshape(b, l, GLA_DV) * jax.nn.silu(g_gla)

    branches = jnp.stack([y_ssd.astype(y_gla.dtype), y_gla], axis=2)
    branch_proj = jnp.einsum('blnc,ncd->blnd', branches, w_branch)
    gates = jax.nn.sigmoid(merge_g.reshape(b, l, N_BRANCH, D_MODEL))
    merged = jnp.sum(gates * branch_proj, axis=2)
    return hres + jnp.einsum('bld,de->ble', merged, w_out)


def setup_inputs(seed: int = 0) -> dict:
    key = jax.random.key(seed)
    ks = jax.random.split(key, 16)
    f32 = jnp.float32
    nrm = jax.random.normal
    x = nrm(ks[0], (BATCH, SEQ, D_MODEL), f32)
    meta_tokens = nrm(ks[1], (N_META, D_MODEL), f32)
    norm_w = 1.0 + 0.02 * nrm(ks[2], (DEPTH, D_MODEL), f32)
    w_in = nrm(ks[3], (DEPTH, D_MODEL, N_IN), f32) * D_MODEL ** -0.5
    conv_w = nrm(ks[4], (DEPTH, SSD_CONV_K, SSD_CONV_DIM), f32) * SSD_CONV_K ** -0.5
    conv_b = 0.02 * nrm(ks[5], (DEPTH, SSD_CONV_DIM), f32)
    dt_init = jnp.exp(jax.random.uniform(ks[6], (DEPTH, SSD_HEADS), f32,
                                         minval=math.log(1e-3), maxval=math.log(1e-1)))
    dt_bias = dt_init + jnp.log(-jnp.expm1(-dt_init))
    a_log = jnp.log(jax.random.uniform(ks[7], (DEPTH, SSD_HEADS), f32, minval=1.0, maxval=16.0))
    d_skip = 1.0 + 0.1 * nrm(ks[8], (DEPTH, SSD_HEADS), f32)
    ssd_norm_w = 1.0 + 0.02 * nrm(ks[9], (DEPTH, D_SSD), f32)
    gla_gate_w2 = nrm(ks[10], (DEPTH, GLA_GATE_RANK, GLA_DK), f32) * GLA_GATE_RANK ** -0.5
    gla_gate_b = 0.5 * nrm(ks[11], (DEPTH, GLA_DK), f32)
    gla_norm_w = 1.0 + 0.02 * nrm(ks[12], (DEPTH, GLA_HEAD_V), f32)
    w_branch = nrm(ks[13], (DEPTH, N_BRANCH, D_BRANCH, D_MODEL), f32) * D_BRANCH ** -0.5
    w_out = nrm(ks[14], (DEPTH, D_MODEL, D_MODEL), f32) * D_MODEL ** -0.5
    final_norm_w = 1.0 + 0.02 * nrm(ks[15], (D_MODEL,), f32)
    return {"x": x, "meta_tokens": meta_tokens, "norm_w": norm_w, "w_in": w_in,
            "conv_w": conv_w, "conv_b": conv_b, "dt_bias": dt_bias, "a_log": a_log,
            "d_skip": d_skip, "ssd_norm_w": ssd_norm_w, "gla_gate_w2": gla_gate_w2,
            "gla_gate_b": gla_gate_b, "gla_norm_w": gla_norm_w, "w_branch": w_branch,
            "w_out": w_out, "final_norm_w": final_norm_w}


def reference(x, meta_tokens, norm_w, w_in, conv_w, conv_b, dt_bias, a_log, d_skip, ssd_norm_w,
              gla_gate_w2, gla_gate_b, gla_norm_w, w_branch, w_out, final_norm_w):
    b = x.shape[0]
    meta = jnp.broadcast_to(meta_tokens.astype(x.dtype)[None], (b, N_META, D_MODEL))
    h = jnp.concatenate([meta, x], axis=1)
    for i in range(DEPTH):
        h = hybrid_layer(h, norm_w[i], w_in[i], conv_w[i], conv_b[i], dt_bias[i], a_log[i],
                         d_skip[i], ssd_norm_w[i], gla_gate_w2[i], gla_gate_b[i], gla_norm_w[i],
                         w_branch[i], w_out[i])
    return rmsnorm(h, final_norm_w)[:, N_META:]
```

```python
import functools

import jax
import jax.numpy as jnp
from jax import lax
from jax.experimental import pallas as pl
from jax.experimental.pallas import tpu as pltpu

F32 = jnp.float32
BF16 = jnp.bfloat16

SSD_HEAD_DIM = 64
SSD_GROUPS = 4
SSD_STATE = 128
GLA_HEADS = 4
GLA_GATE_TEMP = 16.0
EPS = 1e-6

CHUNK = 128
SMALL_W = 128
SUBLANES = 8
TAIL = 8
GLA_SAFE_LOG_DECAY = -60.0
VMEM_LIMIT = 56 * 1024 * 1024


def _sigmoid(x):
    return 0.5 * (jnp.tanh(0.5 * x) + 1.0)


def _silu(x):
    return x * _sigmoid(x)


def _softplus(x):
    return jnp.maximum(x, 0.0) + jnp.log1p(jnp.exp(-jnp.abs(x)))


def _split2(x):
    hi = x.astype(BF16)
    mid = (x - hi.astype(F32)).astype(BF16)
    return hi, mid


def _dot(a, b):
    return jnp.dot(a, b, preferred_element_type=F32)


def _dot_nt(a, b):
    return lax.dot_general(a, b, (((1,), (1,)), ((), ())), preferred_element_type=F32)


def _dot_tn(a, b):
    return lax.dot_general(a, b, (((0,), (0,)), ((), ())), preferred_element_type=F32)


def _cumsum_rows(tri_bf, x):
    hi, mid = _split2(x)
    return _dot(tri_bf, hi) + _dot(tri_bf, mid)


def _tri(n):
    r = lax.broadcasted_iota(jnp.int32, (n, n), 0)
    c = lax.broadcasted_iota(jnp.int32, (n, n), 1)
    return r >= c


def _inproj_body(h_ref, nw_ref, wm_ref, ws_ref, *out_refs):
    main_refs, sm_ref = out_refs[:-1], out_refs[-1]
    x = h_ref[...]
    u = x * lax.rsqrt(jnp.mean(x * x, axis=-1, keepdims=True) + EPS) * nw_ref[...]
    u = u.astype(BF16)
    off = 0
    for ref in main_refs:
        width = ref.shape[-1]
        step = min(width, 512)
        for c0 in range(0, width, step):
            ref[:, c0:c0 + step] = _dot(u, wm_ref[:, off + c0:off + c0 + step]).astype(ref.dtype)
        off += width
    sm_ref[...] = _dot(u, ws_ref[...])


def _inproj(h2d, norm_w, w_main, w_small, widths, tm):
    m, d = h2d.shape
    n_main = w_main.shape[1]
    const = lambda i: (0, 0)
    out_shape = [jax.ShapeDtypeStruct((m, w), BF16) for w in widths]
    out_shape.append(jax.ShapeDtypeStruct((m, SMALL_W), F32))
    out_specs = [pl.BlockSpec((tm, w), lambda i: (i, 0)) for w in widths]
    out_specs.append(pl.BlockSpec((tm, SMALL_W), lambda i: (i, 0)))
    return pl.pallas_call(
        _inproj_body,
        out_shape=out_shape,
        grid=(m // tm,),
        in_specs=[
            pl.BlockSpec((tm, d), lambda i: (i, 0)),
            pl.BlockSpec((1, d), const),
            pl.BlockSpec((d, n_main), const, pipeline_mode=pl.Buffered(1)),
            pl.BlockSpec((d, SMALL_W), const, pipeline_mode=pl.Buffered(1)),
        ],
        out_specs=out_specs,
        compiler_params=pltpu.CompilerParams(
            dimension_semantics=("arbitrary",), vmem_limit_bytes=VMEM_LIMIT),
        name="inproj",
    )(h2d, norm_w, w_main, w_small)


def _ssd_body(xbc_ref, z_ref, sm_ref, cw_ref, cb_ref, dtb_ref, alog_ref, dsk_ref, nw_ref, s0_ref, t0_ref,
              y_ref, sf_ref, tf_ref, state, ext, *, tile, n_valid, heads, d_ssd):
    i = pl.program_id(1)
    n_tiles = pl.num_programs(1)
    gw = d_ssd // SSD_GROUPS
    hpg = heads // SSD_GROUPS
    gs = SSD_STATE

    @pl.when(i == 0)
    def _():
        state[...] = s0_ref[...]
        ext[0:TAIL, :] = t0_ref[...]

    ext[TAIL:TAIL + tile, :] = xbc_ref[0].astype(F32)

    lane_s = lax.broadcasted_iota(jnp.int32, (CHUNK, SMALL_W), 1)
    head_lane = (lane_s < heads) | ((lane_s >= 2 * heads) & (lane_s < 3 * heads))
    causal = _tri(CHUNK)
    tri_bf = causal.astype(BF16)
    er = lax.broadcasted_iota(jnp.int32, (SMALL_W, d_ssd), 0)
    ec = lax.broadcasted_iota(jnp.int32, (SMALL_W, d_ssd), 1) // SSD_HEAD_DIM
    expand = ((er == ec) | (er == ec + 2 * heads)).astype(BF16)
    pick = lax.broadcasted_iota(jnp.int32, (CHUNK, SMALL_W), 1) < 2 * heads
    blk_lane = lax.broadcasted_iota(jnp.int32, (CHUNK, gw), 1) // SSD_HEAD_DIM
    a_full = -jnp.exp(alog_ref[...])

    def expand_heads(x):
        hi, mid = _split2(x)
        return _dot(jnp.where(pick, hi, mid), expand)

    for c in range(tile // CHUNK):
        r0 = c * CHUNK
        xc = (cb_ref[...]
              + cw_ref[3:4, :] * ext[r0 + TAIL:r0 + TAIL + CHUNK, :]
              + cw_ref[2:3, :] * ext[r0 + TAIL - 1:r0 + TAIL - 1 + CHUNK, :]
              + cw_ref[1:2, :] * ext[r0 + TAIL - 2:r0 + TAIL - 2 + CHUNK, :]
              + cw_ref[0:1, :] * ext[r0 + TAIL - 3:r0 + TAIL - 3 + CHUNK, :])
        act = _silu(xc)
        xs = act[:, :d_ssd]
        bm = act[:, d_ssd:d_ssd + SSD_GROUPS * gs].astype(BF16)
        cm = act[:, d_ssd + SSD_GROUPS * gs:].astype(BF16)
        xs_bf = xs.astype(BF16)

        dt = _softplus(sm_ref[0, r0:r0 + CHUNK, :] + dtb_ref[...])
        dt = jnp.where(head_lane, dt, 0.0)
        if n_valid < tile:
            row = lax.broadcasted_iota(jnp.int32, (CHUNK, SMALL_W), 0) + r0
            dt = jnp.where(row < n_valid, dt, 0.0)
        cum = _cumsum_rows(tri_bf, dt * a_full)
        cum_last = cum[CHUNK - 1:CHUNK, :]
        cum_t = cum.T
        dt_t = dt.T

        e_full = expand_heads(jnp.exp(cum))
        w_full = expand_heads(jnp.exp(cum_last - cum) * dt)
        e_last = e_full[CHUNK - 1:CHUNK, :]

        st_bf = state[...].astype(BF16)
        xw = (xs * w_full).astype(BF16)
        y_parts = []
        for g in range(SSD_GROUPS):
            bg = bm[:, g * gs:(g + 1) * gs]
            cg = cm[:, g * gs:(g + 1) * gs]
            lanes = slice(g * gw, (g + 1) * gw)
            cb_g = _dot_nt(cg, bg)
            m_parts, x_parts = [], []
            for j in range(hpg):
                h = g * hpg + j
                seg = cum[:, h:h + 1] - cum_t[h:h + 1, :]
                dec = jnp.where(causal, jnp.exp(jnp.minimum(seg, 0.0)), 0.0)
                m_parts.append((cb_g * dec * dt_t[h:h + 1, :]).astype(BF16))
                x_parts.append(jnp.where(blk_lane == j, xs_bf[:, lanes], jnp.zeros_like(xs_bf[:, lanes])))
            y_diag = _dot(jnp.concatenate(m_parts, axis=1), jnp.concatenate(x_parts, axis=0))
            y_off = _dot(cg, st_bf[:, lanes])
            y_parts.append(y_diag + y_off * e_full[:, lanes])
            state[:, lanes] = state[:, lanes] * e_last[:, lanes] + _dot_tn(bg, xw[:, lanes])
        y = jnp.concatenate(y_parts, axis=1) + dsk_ref[...] * xs
        y = y * _silu(z_ref[0, r0:r0 + CHUNK, :].astype(F32))
        for g in range(SSD_GROUPS):
            lanes = slice(g * gw, (g + 1) * gw)
            yg = y[:, lanes]
            yn = yg * lax.rsqrt(jnp.mean(yg * yg, axis=-1, keepdims=True) + EPS) * nw_ref[:, lanes]
            y_ref[0, r0:r0 + CHUNK, lanes] = yn.astype(y_ref.dtype)

    ext[0:TAIL, :] = ext[tile:tile + TAIL, :]

    @pl.when(i == n_tiles - 1)
    def _():
        sf_ref[0] = state[...]
        tf_ref[0] = ext[n_valid:n_valid + TAIL, :]


def _ssd(xbc, z, small, p, s0, t0, tile, n_valid):
    b, l, cdim = xbc.shape
    d_ssd = z.shape[-1]
    heads = d_ssd // SSD_HEAD_DIM
    n_tiles = l // tile
    tok = lambda bi, i: (bi, i, 0)
    const = lambda bi, i: (0, 0)
    per_b = lambda bi, i: (bi, 0, 0)
    body = functools.partial(_ssd_body, tile=tile, n_valid=n_valid, heads=heads, d_ssd=d_ssd)
    return pl.pallas_call(
        body,
        out_shape=[jax.ShapeDtypeStruct((b, l, d_ssd), BF16),
                   jax.ShapeDtypeStruct((b, SSD_STATE, d_ssd), F32),
                   jax.ShapeDtypeStruct((b, TAIL, cdim), F32)],
        grid=(b, n_tiles),
        in_specs=[
            pl.BlockSpec((1, tile, cdim), tok),
            pl.BlockSpec((1, tile, d_ssd), tok),
            pl.BlockSpec((1, tile, SMALL_W), tok),
            pl.BlockSpec(p["conv_w"].shape, const),
            pl.BlockSpec((1, cdim), const),
            pl.BlockSpec((1, SMALL_W), const),
            pl.BlockSpec((1, SMALL_W), const),
            pl.BlockSpec((1, d_ssd), const),
            pl.BlockSpec((1, d_ssd), const),
            pl.BlockSpec((SSD_STATE, d_ssd), const),
            pl.BlockSpec((TAIL, cdim), const),
        ],
        out_specs=[pl.BlockSpec((1, tile, d_ssd), tok),
                   pl.BlockSpec((1, SSD_STATE, d_ssd), per_b),
                   pl.BlockSpec((1, TAIL, cdim), per_b)],
        scratch_shapes=[pltpu.VMEM((SSD_STATE, d_ssd), F32),
                        pltpu.VMEM((tile + TAIL, cdim), F32)],
        compiler_params=pltpu.CompilerParams(
            dimension_semantics=("arbitrary", "arbitrary"), vmem_limit_bytes=VMEM_LIMIT),
        name="ssd",
    )(xbc, z, small, p["conv_w"], p["conv_b"], p["dt_bias"], p["a_log"], p["d_skip"], p["ssd_norm_w"], s0, t0)


def _gla_body(q_ref, k_ref, v_ref, gg_ref, sm_ref, w2_ref, gb_ref, nw_ref, s0_ref,
              y_ref, sf_ref, state, att_scr, kf_scr, g_scr, *, tile, n_valid, dk, dv):
    i = pl.program_id(1)
    n_tiles = pl.num_programs(1)
    hk = dk // GLA_HEADS
    hv = dv // GLA_HEADS
    scale = hk ** -0.5

    @pl.when(i == 0)
    def _():
        state[...] = s0_ref[...]

    causal = _tri(CHUNK)
    tri_bf = causal.astype(BF16)
    col_id = lax.broadcasted_iota(jnp.int32, (CHUNK, CHUNK), 1)

    for c in range(tile // CHUNK):
        rows = slice(c * CHUNK, (c + 1) * CHUNK)
        a_hi, a_mid = _split2(sm_ref[0, rows, :])
        w_hi, w_mid = _split2(w2_ref[...])
        logit = _dot(a_hi, w_hi) + _dot(a_mid, w_hi) + _dot(a_hi, w_mid) + gb_ref[...]
        log_a = -_softplus(-logit) * (1.0 / GLA_GATE_TEMP)
        kf = k_ref[0, rows, :].astype(F32)
        if n_valid < tile:
            row = lax.broadcasted_iota(jnp.int32, (CHUNK, dk), 0) + c * CHUNK
            log_a = jnp.where(row < n_valid, log_a, 0.0)
            kf = jnp.where(row < n_valid, kf, 0.0)
        g = _cumsum_rows(tri_bf, log_a)
        g_last = g[CHUNK - 1:CHUNK, :]
        qs = q_ref[0, rows, :].astype(F32) * scale
        q_dec = (qs * jnp.exp(g)).astype(BF16)
        k_dec = (kf * jnp.exp(g_last - g)).astype(BF16)
        v_bf = v_ref[0, rows, :]
        safe = jnp.min(g_last) >= GLA_SAFE_LOG_DECAY

        @pl.when(safe)
        def _():
            k_inv = (kf * jnp.exp(-g)).astype(BF16)
            for h in range(GLA_HEADS):
                ks = slice(h * hk, (h + 1) * hk)
                att_scr[h] = _dot_nt(q_dec[:, ks], k_inv[:, ks])

        @pl.when(jnp.logical_not(safe))
        def _():
            kf_scr[...] = kf
            g_scr[...] = g
            for h in range(GLA_HEADS):
                ks = slice(h * hk, (h + 1) * hk)
                q_h, g_h = qs[:, ks], g[:, ks]

                def key_rows(sb, att, q_h=q_h, g_h=g_h, ks=ks):
                    s0 = pl.multiple_of(sb * SUBLANES, SUBLANES)
                    k_rows = kf_scr[pl.ds(s0, SUBLANES), ks]
                    g_rows = g_scr[pl.ds(s0, SUBLANES), ks]
                    for j in range(SUBLANES):
                        p = q_h * k_rows[j:j + 1, :] * jnp.exp(jnp.minimum(g_h - g_rows[j:j + 1, :], 0.0))
                        att = jnp.where(col_id == s0 + j, jnp.sum(p, axis=1, keepdims=True), att)
                    return att

                att_scr[h] = lax.fori_loop(0, CHUNK // SUBLANES, key_rows, jnp.zeros((CHUNK, CHUNK), F32))

        for h in range(GLA_HEADS):
            ks = slice(h * hk, (h + 1) * hk)
            vs = slice(h * hv, (h + 1) * hv)
            att = jnp.where(causal, att_scr[h], 0.0).astype(BF16)
            s_h = state[:, vs]
            o = _dot(jnp.concatenate([q_dec[:, ks], att], axis=1),
                     jnp.concatenate([s_h.astype(BF16), v_bf[:, vs]], axis=0))
            decay_col = jnp.broadcast_to(jnp.exp(g_last[:, ks]), (CHUNK, hk)).T
            decay_col = jnp.concatenate([decay_col] * (hv // hk), axis=1)
            state[:, vs] = s_h * decay_col + _dot_tn(k_dec[:, ks], v_bf[:, vs])
            on = o * lax.rsqrt(jnp.mean(o * o, axis=-1, keepdims=True) + EPS) * nw_ref[...]
            y_ref[0, rows, vs] = (on * _silu(gg_ref[0, rows, vs].astype(F32))).astype(y_ref.dtype)

    @pl.when(i == n_tiles - 1)
    def _():
        sf_ref[0] = state[...]


def _gla(q, k, v, gg, small, p, s0, tile, n_valid):
    b, l, dk = q.shape
    dv = v.shape[-1]
    hk = dk // GLA_HEADS
    n_tiles = l // tile
    tok = lambda bi, i: (bi, i, 0)
    const = lambda bi, i: (0, 0)
    per_b = lambda bi, i: (bi, 0, 0)
    body = functools.partial(_gla_body, tile=tile, n_valid=n_valid, dk=dk, dv=dv)
    return pl.pallas_call(
        body,
        out_shape=[jax.ShapeDtypeStruct((b, l, dv), BF16),
                   jax.ShapeDtypeStruct((b, hk, dv), F32)],
        grid=(b, n_tiles),
        in_specs=[
            pl.BlockSpec((1, tile, dk), tok),
            pl.BlockSpec((1, tile, dk), tok),
            pl.BlockSpec((1, tile, dv), tok),
            pl.BlockSpec((1, tile, dv), tok),
            pl.BlockSpec((1, tile, SMALL_W), tok),
            pl.BlockSpec((SMALL_W, dk), const),
            pl.BlockSpec((1, dk), const),
            pl.BlockSpec((1, dv // GLA_HEADS), const),
            pl.BlockSpec((hk, dv), const),
        ],
        out_specs=[pl.BlockSpec((1, tile, dv), tok),
                   pl.BlockSpec((1, hk, dv), per_b)],
        scratch_shapes=[pltpu.VMEM((hk, dv), F32),
                        pltpu.VMEM((GLA_HEADS, CHUNK, CHUNK), F32),
                        pltpu.VMEM((CHUNK, dk), F32),
                        pltpu.VMEM((CHUNK, dk), F32)],
        compiler_params=pltpu.CompilerParams(
            dimension_semantics=("arbitrary", "arbitrary"), vmem_limit_bytes=VMEM_LIMIT),
        name="gla",
    )(q, k, v, gg, small, p["gate_w2"], p["gate_b"], p["gla_norm_w"], s0)


def _merge_body(ys_ref, yg_ref, mg_ref, h_ref, wb_ref, wo_ref, fnw_ref, o_ref, *, final_norm):
    d = h_ref.shape[-1]
    bp0 = _dot(ys_ref[...], wb_ref[0])
    bp1 = _dot(yg_ref[...], wb_ref[1])
    g0 = _sigmoid(mg_ref[:, :d].astype(F32))
    g1 = _sigmoid(mg_ref[:, d:].astype(F32))
    merged = (g0 * bp0 + g1 * bp1).astype(BF16)
    out = h_ref[...] + _dot(merged, wo_ref[...])
    if final_norm:
        out = out * lax.rsqrt(jnp.mean(out * out, axis=-1, keepdims=True) + EPS) * fnw_ref[...]
    o_ref[...] = out


def _merge(ys, yg, mg, h2d, w_branch, w_out, final_norm_w, tm, final_norm):
    m, d = h2d.shape
    row = lambda i: (i, 0)
    return pl.pallas_call(
        functools.partial(_merge_body, final_norm=final_norm),
        out_shape=jax.ShapeDtypeStruct((m, d), F32),
        grid=(m // tm,),
        in_specs=[
            pl.BlockSpec((tm, d), row),
            pl.BlockSpec((tm, d), row),
            pl.BlockSpec((tm, 2 * d), row),
            pl.BlockSpec((tm, d), row),
            pl.BlockSpec(w_branch.shape, lambda i: (0, 0, 0)),
            pl.BlockSpec(w_out.shape, lambda i: (0, 0)),
            pl.BlockSpec((1, d), lambda i: (0, 0)),
        ],
        out_specs=pl.BlockSpec((tm, d), row),
        compiler_params=pltpu.CompilerParams(
            dimension_semantics=("arbitrary",), vmem_limit_bytes=VMEM_LIMIT),
        name="merge",
    )(ys, yg, mg, h2d, w_branch, w_out, final_norm_w)


def _pad_lanes(v, width):
    v = v.reshape(1, -1)
    return jnp.pad(v, ((0, 0), (0, width - v.shape[1])))


def _layer_params(i, norm_w, w_in, conv_w, conv_b, dt_bias, a_log, d_skip, ssd_norm_w,
                  gla_gate_w2, gla_gate_b, gla_norm_w, w_branch, w_out):
    d = w_out.shape[-1]
    heads = dt_bias.shape[-1]
    rank = gla_gate_w2.shape[1]
    dk = gla_gate_w2.shape[2]
    dv = d
    cdim = conv_w.shape[-1]
    widths = (d, cdim, heads, dk, dk, dv, dv, rank, 2 * d)
    starts = [0]
    for w in widths:
        starts.append(starts[-1] + w)
    col = lambda j: w_in[i][:, starts[j]:starts[j + 1]]
    w_main = jnp.concatenate([col(0), col(1), col(3), col(4), col(5), col(6), col(8)], axis=1).astype(BF16)
    assert rank == heads and 3 * heads <= SMALL_W
    w_small = jnp.concatenate(
        [col(2), col(7), col(2), jnp.zeros((d, SMALL_W - 3 * heads), F32)], axis=1).astype(BF16)
    dtb = jnp.concatenate([dt_bias[i], jnp.zeros((heads,), F32), dt_bias[i]])
    alog = jnp.concatenate([a_log[i], jnp.zeros((heads,), F32), a_log[i]])
    w2 = jnp.zeros((SMALL_W, dk), F32).at[heads:heads + rank].set(gla_gate_w2[i])
    return dict(
        main_widths=(d, cdim, dk, dk, dv, dv, 2 * d),
        norm_w=norm_w[i].reshape(1, d), w_main=w_main, w_small=w_small,
        conv_w=conv_w[i], conv_b=conv_b[i].reshape(1, cdim),
        dt_bias=_pad_lanes(dtb, SMALL_W), a_log=_pad_lanes(alog, SMALL_W),
        d_skip=jnp.repeat(d_skip[i], SSD_HEAD_DIM).reshape(1, d),
        ssd_norm_w=ssd_norm_w[i].reshape(1, d),
        gate_w2=w2, gate_b=gla_gate_b[i].reshape(1, dk), gla_norm_w=gla_norm_w[i].reshape(1, -1),
        w_branch=w_branch[i].astype(BF16), w_out=w_out[i].astype(BF16))


def _layer(h, p, states, tile, tm, n_valid, final_norm_w, final_norm):
    b, l, d = h.shape
    h2d = h.reshape(b * l, d)
    z, xbc, q, k, v, gg, mg, small = _inproj(h2d, p["norm_w"], p["w_main"], p["w_small"], p["main_widths"], tm)
    r3 = lambda t: t.reshape(b, l, t.shape[-1])
    s0, t0, g0 = states
    y_ssd, s_f, t_f = _ssd(r3(xbc), r3(z), r3(small), p, s0, t0, tile, n_valid)
    y_gla, g_f = _gla(r3(q), r3(k), r3(v), r3(gg), r3(small), p, g0, tile, n_valid)
    out = _merge(y_ssd.reshape(b * l, d), y_gla.reshape(b * l, d), mg, h2d,
                 p["w_branch"], p["w_out"], final_norm_w, tm, final_norm)
    return out.reshape(b, l, d), (s_f[0], t_f[0], g_f[0])


def kernel(x, meta_tokens, norm_w, w_in, conv_w, conv_b, dt_bias, a_log, d_skip, ssd_norm_w, gla_gate_w2,
           gla_gate_b, gla_norm_w, w_branch, w_out, final_norm_w):
    b, seq, d = x.shape
    depth = norm_w.shape[0]
    n_meta = meta_tokens.shape[0]
    assert n_meta <= CHUNK and seq % CHUNK == 0
    tile = 256 if seq % 256 == 0 else CHUNK
    tm = 512 if (b * seq) % 512 == 0 else CHUNK
    cdim = conv_w.shape[-1]
    dk = gla_gate_w2.shape[2]
    fnw = final_norm_w.reshape(1, d)
    zero_states = (jnp.zeros((SSD_STATE, d), F32), jnp.zeros((TAIL, cdim), F32),
                   jnp.zeros((dk // GLA_HEADS, d), F32))
    h_meta = jnp.pad(meta_tokens.astype(x.dtype), ((0, CHUNK - n_meta), (0, 0)))[None]
    h = x
    for i in range(depth):
        p = _layer_params(i, norm_w, w_in, conv_w, conv_b, dt_bias, a_log, d_skip, ssd_norm_w,
                          gla_gate_w2, gla_gate_b, gla_norm_w, w_branch, w_out)
        h_meta, seeded = _layer(h_meta, p, zero_states, CHUNK, CHUNK, n_meta, fnw, False)
        h, _ = _layer(h, p, seeded, tile, tm, tile, fnw, i == depth - 1)
    return h
```

```python
import functools

import jax
import jax.numpy as jnp
from jax import lax
from jax.experimental import pallas as pl
from jax.experimental.pallas import tpu as pltpu

F32 = jnp.float32
BF16 = jnp.bfloat16

SSD_HEAD_DIM = 64
SSD_GROUPS = 4
SSD_STATE = 128
GLA_HEADS = 4
GLA_GATE_TEMP = 16.0
EPS = 1e-6

CHUNK = 128
SMALL_W = 128
SUBLANES = 8
TAIL = 16
PROJ_LANES = 512
NEG_BIG = -1e30
GLA_SAFE_LOG_DECAY = -60.0
VMEM_LIMIT = 56 * 1024 * 1024


def _sigmoid(x):
    return 0.5 * (jnp.tanh(0.5 * x) + 1.0)


def _silu(x):
    h = 0.5 * x
    return h + h * jnp.tanh(h)


def _softplus(x):
    return jnp.maximum(x, 0.0) + jnp.log(1.0 + jnp.exp(-jnp.abs(x)))


def _rmsnorm(x, w):
    return x * lax.rsqrt(jnp.mean(x * x, axis=-1, keepdims=True) + EPS) * w


def _split2(x):
    hi = x.astype(BF16)
    mid = (x - hi.astype(F32)).astype(BF16)
    return hi, mid


def _dot(a, b):
    return jnp.dot(a, b, preferred_element_type=F32)


def _dot_nt(a, b):
    return lax.dot_general(a, b, (((1,), (1,)), ((), ())), preferred_element_type=F32)


def _dot_tn(a, b):
    return lax.dot_general(a, b, (((0,), (0,)), ((), ())), preferred_element_type=F32)


def _cumsum_rows(tri_bf, x):
    hi, mid = _split2(x)
    return _dot(tri_bf, hi) + _dot(tri_bf, mid)


def _round_robin(generators, fillers):
    generators = list(generators)
    while generators or fillers:
        for gen in list(generators):
            if next(gen, StopIteration) is StopIteration:
                generators.remove(gen)
        if fillers:
            fillers.pop(0)()


def _ssd_chunk(xbc, z, sm, prm, consts, y, rows, row0, state, tail, *, n_valid):
    cw_ref, cb_ref, dtb_ref, alog_ref, dsk_ref, nw_ref = prm
    expand_ref, tri_ref, negmask_ref = consts
    d_ssd = z.shape[-1]
    cdim = xbc.shape[-1]
    heads = d_ssd // SSD_HEAD_DIM
    gw = d_ssd // SSD_GROUPS
    hpg = heads // SSD_GROUPS
    gs = SSD_STATE
    n_taps = cw_ref.shape[0]

    def conv_silu(lanes):
        cur = xbc[:, lanes].astype(F32)
        hist = tail[:, lanes].astype(F32)[TAIL - SUBLANES:, :]
        ext = jnp.concatenate([hist, cur], axis=0)
        xc = cb_ref[:, lanes] + cw_ref[n_taps - 1:n_taps, lanes] * cur
        for j in range(1, n_taps):
            xc = xc + cw_ref[n_taps - 1 - j:n_taps - j, lanes] * pltpu.roll(ext, j, axis=0)[SUBLANES:, :]
        return _silu(xc)

    bm = conv_silu(slice(d_ssd, d_ssd + SSD_GROUPS * gs)).astype(BF16)
    yield
    cm = conv_silu(slice(d_ssd + SSD_GROUPS * gs, cdim)).astype(BF16)
    yield

    lane_s = lax.broadcasted_iota(jnp.int32, (CHUNK, SMALL_W), 1)
    head_lane = (lane_s < heads) | ((lane_s >= 2 * heads) & (lane_s < 3 * heads))
    dt = _softplus(sm[...] + dtb_ref[...])
    dt = jnp.where(head_lane, dt, 0.0)
    if n_valid is not None:
        row = lax.broadcasted_iota(jnp.int32, (CHUNK, SMALL_W), 0) + row0
        dt = jnp.where(row < n_valid, dt, 0.0)
    a_full = -jnp.exp(alog_ref[...])
    cum = _cumsum_rows(tri_ref[...], dt * a_full)
    cum_last = cum[CHUNK - 1:CHUNK, :]
    cum_t = cum.T
    dt_t = dt.T

    def head_split(v):
        hi, mid = _split2(v)
        return jnp.where(lane_s < 2 * heads, hi, mid)

    e_split = head_split(jnp.exp(cum))
    w_split = head_split(jnp.exp(cum_last - cum) * dt)
    yield

    blk_lane = lax.broadcasted_iota(jnp.int32, (CHUNK, gw), 1) // SSD_HEAD_DIM
    for g in range(SSD_GROUPS):
        lanes = slice(g * gw, (g + 1) * gw)
        bg = bm[:, g * gs:(g + 1) * gs]
        cg = cm[:, g * gs:(g + 1) * gs]
        xs = conv_silu(lanes)
        xs_bf = xs.astype(BF16)
        e_g = _dot(e_split, expand_ref[:, lanes])
        w_g = _dot(w_split, expand_ref[:, lanes])
        xw = (xs * w_g).astype(BF16)
        cb_g = _dot_nt(cg, bg)
        yield
        m_parts, x_parts = [], []
        for j in range(hpg):
            h = g * hpg + j
            seg = (cum[:, h:h + 1] - cum_t[h:h + 1, :]) + negmask_ref[...]
            m_parts.append((cb_g * jnp.exp(seg) * dt_t[h:h + 1, :]).astype(BF16))
            x_parts.append(jnp.where(blk_lane == j, xs_bf, jnp.zeros_like(xs_bf)))
            if j % 2 == 1:
                yield
        y_diag = _dot(jnp.concatenate(m_parts, axis=1), jnp.concatenate(x_parts, axis=0))
        s_g = state[:, lanes]
        y_off = _dot(cg, s_g.astype(BF16))
        state[:, lanes] = s_g * e_g[CHUNK - 1:CHUNK, :] + _dot_tn(bg, xw)
        yg = y_diag + y_off * e_g + dsk_ref[:, lanes] * xs
        yg = yg * _silu(z[:, lanes].astype(F32))
        y[rows, lanes] = _rmsnorm(yg, nw_ref[:, lanes]).astype(y.dtype)
        yield

    tail[...] = xbc[CHUNK - TAIL:CHUNK, :]


class _GlaChunk:
    def __init__(self, q, k, v, gg, sm, prm, consts, y, rows, row0, s_old, s_new, kf_scr, g_scr, *, n_valid):
        self.q, self.k, self.v, self.gg, self.sm = q, k, v, gg, sm
        self.w2_ref, self.gb_ref, self.nw_ref = prm
        self.tri_ref, self.tri01_ref = consts
        self.y, self.rows, self.row0 = y, rows, row0
        self.s_old, self.s_new = s_old, s_new
        self.kf_scr, self.g_scr = kf_scr, g_scr
        self.n_valid = n_valid
        self.hk = q.shape[-1] // GLA_HEADS
        self.hv = v.shape[-1] // GLA_HEADS
        self.g_mins = []

    def _lanes(self, h):
        return slice(h * self.hk, (h + 1) * self.hk), slice(h * self.hv, (h + 1) * self.hv)

    def _gate_decay(self, ks):
        logit = _dot(self.sm[...].astype(BF16), self.w2_ref[:, ks].astype(BF16)) + self.gb_ref[:, ks]
        log_a = -_softplus(-logit) * (1.0 / GLA_GATE_TEMP)
        kf = self.k[:, ks].astype(F32)
        if self.n_valid is not None:
            row = lax.broadcasted_iota(jnp.int32, (CHUNK, self.hk), 0) + self.row0
            log_a = jnp.where(row < self.n_valid, log_a, 0.0)
            kf = jnp.where(row < self.n_valid, kf, 0.0)
        qs = self.q[:, ks].astype(F32) * (self.hk ** -0.5)
        return qs, kf, _cumsum_rows(self.tri_ref[...], log_a)

    def _emit(self, h, q_dec, att):
        ks, vs = self._lanes(h)
        att = (att * self.tri01_ref[...]).astype(BF16)
        o = _dot(jnp.concatenate([q_dec, att], axis=1),
                 jnp.concatenate([self.s_old[:, vs].astype(BF16), self.v[:, vs]], axis=0))
        gate = _silu(self.gg[:, vs].astype(F32))
        self.y[self.rows, vs] = (_rmsnorm(o, self.nw_ref[...]) * gate).astype(self.y.dtype)

    def stages(self):
        for h in range(GLA_HEADS):
            ks, vs = self._lanes(h)
            qs, kf, g = self._gate_decay(ks)
            yield
            g_last = g[CHUNK - 1:CHUNK, :]
            q_dec = (qs * jnp.exp(g)).astype(BF16)
            k_inv = (kf * jnp.exp(jnp.minimum(-g, -GLA_SAFE_LOG_DECAY))).astype(BF16)
            k_dec = (kf * jnp.exp(g_last - g)).astype(BF16)
            att = _dot_nt(q_dec, k_inv)
            self.g_mins.append(jnp.min(g_last))
            yield
            self._emit(h, q_dec, att)
            decay_col = jnp.broadcast_to(jnp.exp(g_last), (CHUNK, self.hk)).T
            decay_col = jnp.concatenate([decay_col] * (self.hv // self.hk), axis=1)
            self.s_new[:, vs] = self.s_old[:, vs] * decay_col + _dot_tn(k_dec, self.v[:, vs])
            yield

    def redo_unsafe(self):
        g_min = functools.reduce(jnp.minimum, self.g_mins)

        @pl.when(g_min < GLA_SAFE_LOG_DECAY)
        def _():
            col_id = lax.broadcasted_iota(jnp.int32, (CHUNK, CHUNK), 1)
            for h in range(GLA_HEADS):
                ks, _ = self._lanes(h)
                q_h, kf, g_h = self._gate_decay(ks)
                self.kf_scr[...] = kf
                self.g_scr[...] = g_h

                def key_rows(sb, att, q_h=q_h, g_h=g_h):
                    s0 = pl.multiple_of(sb * SUBLANES, SUBLANES)
                    k_rows = self.kf_scr[pl.ds(s0, SUBLANES), :]
                    g_rows = self.g_scr[pl.ds(s0, SUBLANES), :]
                    for j in range(SUBLANES):
                        p = q_h * k_rows[j:j + 1, :] * jnp.exp(jnp.minimum(g_h - g_rows[j:j + 1, :], 0.0))
                        att = jnp.where(col_id == s0 + j, jnp.sum(p, axis=1, keepdims=True), att)
                    return att

                att = lax.fori_loop(0, CHUNK // SUBLANES, key_rows, jnp.zeros((CHUNK, CHUNK), F32))
                self._emit(h, (q_h * jnp.exp(g_h)).astype(BF16), att)


def _mixer_body(h_ref, hn_ref, nw_ref, wm_ref, ws_ref,
                cw_ref, cb_ref, dtb_ref, alog_ref, dsk_ref, snw_ref, s0_ref, t0_ref,
                w2_ref, gb_ref, gnw_ref, g0_ref,
                ys_ref, yg_ref, mg_ref, sf_ref, tf_ref, gf_ref,
                u_s, z_s, xbc_s, q_s, k_s, v_s, gg_s, sm_s, ssd_state, tail, gla_state, kf_scr, g_scr,
                expand_s, tri_s, tri01_s, negmask_s,
                *, tile, n_valid, tiles_per_seq):
    i = pl.program_id(0)
    n_chunks = tile // CHUNK
    d = h_ref.shape[-1]
    heads = d // SSD_HEAD_DIM

    slots = (("z", z_s), ("xbc", xbc_s), ("q", q_s), ("k", k_s), ("v", v_s), ("gg", gg_s))
    off = {}
    start = 0
    for name, ref in slots:
        off[name] = start
        start += ref.shape[-1]
    off_mg = start

    def project_jobs(slot):
        jobs = []

        def small():
            sm_s[slot] = _dot(u_s[slot], ws_ref[...])
        jobs.append(small)
        for name in ("xbc", "z", "q", "k", "v", "gg"):
            ref = dict(slots)[name]
            width = ref.shape[-1]
            step = min(width, PROJ_LANES)
            for c0 in range(0, width, step):
                def job(ref=ref, c0=c0, step=step, o=off[name]):
                    ref[slot, :, c0:c0 + step] = _dot(u_s[slot], wm_ref[:, o + c0:o + c0 + step]).astype(ref.dtype)
                jobs.append(job)
        return jobs

    def merge_gate_jobs(slot, rows):
        jobs = []
        width = mg_ref.shape[-1]
        for c0 in range(0, width, PROJ_LANES):
            def job(c0=c0):
                mg_ref[rows, c0:c0 + PROJ_LANES] = _dot(
                    u_s[slot], wm_ref[:, off_mg + c0:off_mg + c0 + PROJ_LANES]).astype(mg_ref.dtype)
            jobs.append(job)
        return jobs

    @pl.when(i == 0)
    def _():
        er = lax.broadcasted_iota(jnp.int32, expand_s.shape, 0)
        ec = lax.broadcasted_iota(jnp.int32, expand_s.shape, 1) // SSD_HEAD_DIM
        expand_s[...] = ((er == ec) | (er == ec + 2 * heads)).astype(BF16)
        tr = lax.broadcasted_iota(jnp.int32, (CHUNK, CHUNK), 0)
        tc = lax.broadcasted_iota(jnp.int32, (CHUNK, CHUNK), 1)
        tri_s[...] = (tr >= tc).astype(BF16)
        tri01_s[...] = (tr >= tc).astype(F32)
        negmask_s[...] = jnp.where(tr >= tc, 0.0, NEG_BIG)
        u_s[0] = _rmsnorm(h_ref[0:CHUNK, :], nw_ref[...]).astype(BF16)
        for job in project_jobs(0):
            job()

    @pl.when(i % tiles_per_seq == 0)
    def _():
        ssd_state[...] = s0_ref[...]
        tail[...] = t0_ref[...]
        gla_state[0] = g0_ref[...]

    ssd_prm = (cw_ref, cb_ref, dtb_ref, alog_ref, dsk_ref, snw_ref)
    ssd_consts = (expand_s, tri_s, negmask_s)
    gla_prm = (w2_ref, gb_ref, gnw_ref)
    gla_consts = (tri_s, tri01_s)
    nv = None if n_valid == tile else n_valid

    def chunk_step(c, cur):
        nxt = 1 - cur
        row0 = pl.multiple_of(c * CHUNK, CHUNK)
        rows = pl.ds(row0, CHUNK)
        r_next = pl.multiple_of(jnp.minimum(row0 + CHUNK, tile - CHUNK), CHUNK)
        x_next = jnp.where(c == n_chunks - 1, hn_ref[...], h_ref[pl.ds(r_next, CHUNK), :])
        u_s[nxt] = _rmsnorm(x_next, nw_ref[...]).astype(BF16)
        gla = _GlaChunk(q_s.at[cur], k_s.at[cur], v_s.at[cur], gg_s.at[cur], sm_s.at[cur], gla_prm, gla_consts,
                        yg_ref, rows, row0, gla_state.at[cur], gla_state.at[nxt], kf_scr, g_scr, n_valid=nv)
        ssd = _ssd_chunk(xbc_s.at[cur], z_s.at[cur], sm_s.at[cur], ssd_prm, ssd_consts, ys_ref, rows, row0,
                         ssd_state, tail, n_valid=nv)
        _round_robin([ssd, gla.stages()], project_jobs(nxt) + merge_gate_jobs(cur, rows))
        gla.redo_unsafe()

    if n_chunks == 1:
        chunk_step(0, 0)
        last_slot = 1
    else:
        assert n_chunks % 2 == 0

        def pair_step(p, carry):
            chunk_step(2 * p, 0)
            chunk_step(2 * p + 1, 1)
            return carry

        lax.fori_loop(0, n_chunks // 2, pair_step, 0)
        last_slot = 0

    if sf_ref is not None:
        @pl.when(i % tiles_per_seq == tiles_per_seq - 1)
        def _():
            sf_ref[0] = ssd_state[...]
            tf_ref[0] = xbc_s[0, n_valid - TAIL:n_valid, :]
            gf_ref[0] = gla_state[last_slot]


def _mixer_body_no_state(*refs, **kw):
    n_in, n_out = 17, 3
    ins, outs, scratch = refs[:n_in], refs[n_in:n_in + n_out], refs[n_in + n_out:]
    _mixer_body(*ins, *outs, None, None, None, *scratch, **kw)


def _mixer(h, p, states, tile, n_valid, emit_state):
    b, l, d = h.shape
    cdim = p["conv_w"].shape[-1]
    assert p["conv_w"].shape[0] - 1 <= SUBLANES
    dk = p["gate_b"].shape[-1]
    dv = d
    hk = dk // GLA_HEADS
    n_main = p["w_main"].shape[1]
    tiles_per_seq = l // tile
    n_tiles = b * tiles_per_seq
    chunks_per_tile = tile // CHUNK
    n_chunks_total = b * l // CHUNK
    assert not emit_state or tile == CHUNK
    s0, t0, g0 = states
    h2d = h.reshape(b * l, d)
    tok = lambda i: (i, 0)
    nxt = lambda i: (jnp.minimum((i + 1) * chunks_per_tile, n_chunks_total - 1), 0)
    const = lambda i: (0, 0)
    per_b = lambda i: (i // tiles_per_seq, 0, 0)
    whole = lambda shape: pl.BlockSpec(shape, const)
    resident = lambda shape: pl.BlockSpec(shape, const, pipeline_mode=pl.Buffered(1))
    out_shape = [jax.ShapeDtypeStruct((b * l, d), BF16),
                 jax.ShapeDtypeStruct((b * l, dv), BF16),
                 jax.ShapeDtypeStruct((b * l, 2 * d), BF16)]
    out_specs = [pl.BlockSpec((tile, d), tok), pl.BlockSpec((tile, dv), tok), pl.BlockSpec((tile, 2 * d), tok)]
    if emit_state:
        out_shape += [jax.ShapeDtypeStruct((b, SSD_STATE, d), F32),
                      jax.ShapeDtypeStruct((b, TAIL, cdim), BF16),
                      jax.ShapeDtypeStruct((b, hk, dv), F32)]
        out_specs += [pl.BlockSpec((1, SSD_STATE, d), per_b),
                      pl.BlockSpec((1, TAIL, cdim), per_b),
                      pl.BlockSpec((1, hk, dv), per_b)]
    kw = dict(tile=tile, n_valid=n_valid, tiles_per_seq=tiles_per_seq)
    body = functools.partial(_mixer_body if emit_state else _mixer_body_no_state, **kw)
    return pl.pallas_call(
        body,
        out_shape=out_shape,
        grid=(n_tiles,),
        in_specs=[
            pl.BlockSpec((tile, d), tok),
            pl.BlockSpec((CHUNK, d), nxt),
            whole((1, d)),
            resident((d, n_main)),
            resident((d, SMALL_W)),
            whole(p["conv_w"].shape), whole((1, cdim)), whole((1, SMALL_W)), whole((1, SMALL_W)),
            whole((1, d)), whole((1, d)), whole((SSD_STATE, d)), whole((TAIL, cdim)),
            whole((SMALL_W, dk)), whole((1, dk)), whole((1, dv // GLA_HEADS)), whole((hk, dv)),
        ],
        out_specs=out_specs,
        scratch_shapes=[pltpu.VMEM((2, CHUNK, d), BF16),
                        pltpu.VMEM((2, CHUNK, d), BF16),
                        pltpu.VMEM((2, CHUNK, cdim), BF16),
                        pltpu.VMEM((2, CHUNK, dk), BF16),
                        pltpu.VMEM((2, CHUNK, dk), BF16),
                        pltpu.VMEM((2, CHUNK, dv), BF16),
                        pltpu.VMEM((2, CHUNK, dv), BF16),
                        pltpu.VMEM((2, CHUNK, SMALL_W), F32),
                        pltpu.VMEM((SSD_STATE, d), F32),
                        pltpu.VMEM((TAIL, cdim), BF16),
                        pltpu.VMEM((2, hk, dv), F32),
                        pltpu.VMEM((CHUNK, hk), F32),
                        pltpu.VMEM((CHUNK, hk), F32),
                        pltpu.VMEM((SMALL_W, d), BF16),
                        pltpu.VMEM((CHUNK, CHUNK), BF16),
                        pltpu.VMEM((CHUNK, CHUNK), F32),
                        pltpu.VMEM((CHUNK, CHUNK), F32)],
        compiler_params=pltpu.CompilerParams(
            dimension_semantics=("arbitrary",), vmem_limit_bytes=VMEM_LIMIT),
        name="mixer",
    )(h2d, h2d, p["norm_w"], p["w_main"], p["w_small"],
      p["conv_w"], p["conv_b"], p["dt_bias"], p["a_log"], p["d_skip"], p["ssd_norm_w"], s0, t0,
      p["gate_w2"], p["gate_b"], p["gla_norm_w"], g0)


def _merge_body(ys_ref, yg_ref, mg_ref, h_ref, wb_ref, wo_ref, fnw_ref, o_ref, *, final_norm):
    d = h_ref.shape[-1]
    bp0 = _dot(ys_ref[...], wb_ref[0])
    bp1 = _dot(yg_ref[...], wb_ref[1])
    g0 = _sigmoid(mg_ref[:, :d].astype(F32))
    g1 = _sigmoid(mg_ref[:, d:].astype(F32))
    merged = (g0 * bp0 + g1 * bp1).astype(BF16)
    out = h_ref[...] + _dot(merged, wo_ref[...])
    if final_norm:
        out = _rmsnorm(out, fnw_ref[...])
    o_ref[...] = out


def _merge(ys, yg, mg, h2d, w_branch, w_out, final_norm_w, tm, final_norm):
    m, d = h2d.shape
    row = lambda i: (i, 0)
    return pl.pallas_call(
        functools.partial(_merge_body, final_norm=final_norm),
        out_shape=jax.ShapeDtypeStruct((m, d), F32),
        grid=(m // tm,),
        in_specs=[
            pl.BlockSpec((tm, d), row),
            pl.BlockSpec((tm, d), row),
            pl.BlockSpec((tm, 2 * d), row),
            pl.BlockSpec((tm, d), row),
            pl.BlockSpec(w_branch.shape, lambda i: (0, 0, 0)),
            pl.BlockSpec(w_out.shape, lambda i: (0, 0)),
            pl.BlockSpec((1, d), lambda i: (0, 0)),
        ],
        out_specs=pl.BlockSpec((tm, d), row),
        compiler_params=pltpu.CompilerParams(
            dimension_semantics=("arbitrary",), vmem_limit_bytes=VMEM_LIMIT),
        name="merge",
    )(ys, yg, mg, h2d, w_branch, w_out, final_norm_w)


def _pad_lanes(v, width):
    v = v.reshape(1, -1)
    return jnp.pad(v, ((0, 0), (0, width - v.shape[1])))


def _layer_params(i, norm_w, w_in, conv_w, conv_b, dt_bias, a_log, d_skip, ssd_norm_w,
                  gla_gate_w2, gla_gate_b, gla_norm_w, w_branch, w_out):
    d = w_out.shape[-1]
    heads = dt_bias.shape[-1]
    rank = gla_gate_w2.shape[1]
    dk = gla_gate_w2.shape[2]
    dv = d
    cdim = conv_w.shape[-1]
    widths = (d, cdim, heads, dk, dk, dv, dv, rank, 2 * d)
    starts = [0]
    for w in widths:
        starts.append(starts[-1] + w)
    col = lambda j: w_in[i][:, starts[j]:starts[j + 1]]
    w_main = jnp.concatenate([col(0), col(1), col(3), col(4), col(5), col(6), col(8)], axis=1).astype(BF16)
    assert rank == heads and 3 * heads <= SMALL_W
    w_small = jnp.concatenate(
        [col(2), col(7), col(2), jnp.zeros((d, SMALL_W - 3 * heads), F32)], axis=1).astype(BF16)
    dtb = jnp.concatenate([dt_bias[i], jnp.zeros((heads,), F32), dt_bias[i]])
    alog = jnp.concatenate([a_log[i], jnp.zeros((heads,), F32), a_log[i]])
    w2 = jnp.zeros((SMALL_W, dk), F32).at[heads:heads + rank].set(gla_gate_w2[i])
    return dict(
        norm_w=norm_w[i].reshape(1, d), w_main=w_main, w_small=w_small,
        conv_w=conv_w[i], conv_b=conv_b[i].reshape(1, cdim),
        dt_bias=_pad_lanes(dtb, SMALL_W), a_log=_pad_lanes(alog, SMALL_W),
        d_skip=jnp.repeat(d_skip[i], SSD_HEAD_DIM).reshape(1, d),
        ssd_norm_w=ssd_norm_w[i].reshape(1, d),
        gate_w2=w2, gate_b=gla_gate_b[i].reshape(1, dk), gla_norm_w=gla_norm_w[i].reshape(1, -1),
        w_branch=w_branch[i].astype(BF16), w_out=w_out[i].astype(BF16))


def _layer(h, p, states, tile, tm, n_valid, final_norm_w, final_norm, emit_state):
    b, l, d = h.shape
    outs = _mixer(h, p, states, tile, n_valid, emit_state)
    y_ssd, y_gla, mg = outs[:3]
    out = _merge(y_ssd, y_gla, mg, h.reshape(b * l, d), p["w_branch"], p["w_out"], final_norm_w, tm, final_norm)
    final = (outs[3][0], outs[4][0], outs[5][0]) if emit_state else None
    return out.reshape(b, l, d), final


def kernel(x, meta_tokens, norm_w, w_in, conv_w, conv_b, dt_bias, a_log, d_skip, ssd_norm_w, gla_gate_w2,
           gla_gate_b, gla_norm_w, w_branch, w_out, final_norm_w):
    b, seq, d = x.shape
    depth = norm_w.shape[0]
    n_meta = meta_tokens.shape[0]
    assert TAIL <= n_meta <= CHUNK and n_meta % TAIL == 0 and seq % CHUNK == 0
    tile = 512 if seq % 512 == 0 else 2 * CHUNK
    tm = 512 if (b * seq) % 512 == 0 else CHUNK
    cdim = conv_w.shape[-1]
    dk = gla_gate_w2.shape[2]
    fnw = final_norm_w.reshape(1, d)
    zero_states = (jnp.zeros((SSD_STATE, d), F32), jnp.zeros((TAIL, cdim), BF16),
                   jnp.zeros((dk // GLA_HEADS, d), F32))
    h_meta = jnp.pad(meta_tokens.astype(x.dtype), ((0, CHUNK - n_meta), (0, 0)))[None]
    h = x
    for i in range(depth):
        p = _layer_params(i, norm_w, w_in, conv_w, conv_b, dt_bias, a_log, d_skip, ssd_norm_w,
                          gla_gate_w2, gla_gate_b, gla_norm_w, w_branch, w_out)
        h_meta, seeded = _layer(h_meta, p, zero_states, CHUNK, CHUNK, n_meta, fnw, False, True)
        h, _ = _layer(h, p, seeded, tile, tm, tile, fnw, i == depth - 1, False)
    return h
```

```python
import functools

import jax
import jax.numpy as jnp
from jax import lax
from jax.experimental import pallas as pl
from jax.experimental.pallas import tpu as pltpu

F32 = jnp.float32
BF16 = jnp.bfloat16

SSD_HEAD_DIM = 64
SSD_GROUPS = 4
SSD_STATE = 128
GLA_HEADS = 4
GLA_GATE_TEMP = 16.0
EPS = 1e-6

CHUNK = 128
SMALL_W = 128
SUBLANES = 8
TAIL = 16
PROJ_LANES = 512
NEG_BIG = -1e30
GLA_SAFE_LOG_DECAY = -60.0
FILL_STRIDE = 3
VMEM_LIMIT = 60 * 1024 * 1024


def _sigmoid(x):
    return 0.5 * (jnp.tanh(0.5 * x) + 1.0)


def _silu(x):
    h = 0.5 * x
    return h + h * jnp.tanh(h)


def _softplus(x):
    return jnp.maximum(x, 0.0) + jnp.log(1.0 + jnp.exp(-jnp.abs(x)))


def _rmsnorm(x, w):
    return x * lax.rsqrt(jnp.mean(x * x, axis=-1, keepdims=True) + EPS) * w


def _split2(x):
    hi = x.astype(BF16)
    mid = (x - hi.astype(F32)).astype(BF16)
    return hi, mid


def _dot(a, b):
    return jnp.dot(a, b, preferred_element_type=F32)


def _dot_nt(a, b):
    return lax.dot_general(a, b, (((1,), (1,)), ((), ())), preferred_element_type=F32)


def _dot_tn(a, b):
    return lax.dot_general(a, b, (((0,), (0,)), ((), ())), preferred_element_type=F32)


def _cumsum_rows(tri_bf, x):
    hi, mid = _split2(x)
    return _dot(tri_bf, hi) + _dot(tri_bf, mid)


def _round_robin(generators, fillers, n_fill, stride):
    generators = list(generators)
    n_fill = min(n_fill, len(fillers))
    rnd = 0
    while generators or n_fill:
        for gen in list(generators):
            if next(gen, StopIteration) is StopIteration:
                generators.remove(gen)
        if n_fill and (rnd % stride == 0 or not generators):
            fillers.pop(0)()
            n_fill -= 1
        rnd += 1


def _ssd_decays(sm, dtb_ref, alog_ref, tri_ref, out, *, n_chunks, heads, n_valid):
    lane_s = lax.broadcasted_iota(jnp.int32, (CHUNK, SMALL_W), 1)
    head_lane = (lane_s < heads) | ((lane_s >= 2 * heads) & (lane_s < 3 * heads))
    a_full = -jnp.exp(alog_ref[...])
    dts = []
    for c in range(n_chunks):
        dt = _softplus(sm[c * CHUNK:(c + 1) * CHUNK, :] + dtb_ref[...])
        dt = jnp.where(head_lane, dt, 0.0)
        if n_valid is not None:
            row = lax.broadcasted_iota(jnp.int32, (CHUNK, SMALL_W), 0) + c * CHUNK
            dt = jnp.where(row < n_valid, dt, 0.0)
        dts.append(dt)
    yield
    cum_wide = _cumsum_rows(tri_ref[...], jnp.concatenate([dt * a_full for dt in dts], axis=1))
    yield

    def head_split(v):
        hi, mid = _split2(v)
        return jnp.where(lane_s < 2 * heads, hi, mid)

    for c, dt in enumerate(dts):
        cum = cum_wide[:, c * SMALL_W:(c + 1) * SMALL_W]
        cum_last = cum[CHUNK - 1:CHUNK, :]
        out.append((cum, cum.T, dt.T, head_split(jnp.exp(cum)), head_split(jnp.exp(cum_last - cum) * dt)))
    yield


def _ssd_chunk(xbc, z, decay, prm, consts, y, rows, state, tail):
    cw_ref, cb_ref, dsk_ref, nw_ref = prm
    expand_ref, negmask_ref = consts
    d_ssd = z.shape[-1]
    cdim = xbc.shape[-1]
    heads = d_ssd // SSD_HEAD_DIM
    gw = d_ssd // SSD_GROUPS
    hpg = heads // SSD_GROUPS
    gs = SSD_STATE
    n_taps = cw_ref.shape[0]

    def conv_silu(lanes):
        cur = xbc[:, lanes].astype(F32)
        hist = tail[:, lanes].astype(F32)[TAIL - SUBLANES:, :]
        ext = jnp.concatenate([hist, cur], axis=0)
        xc = cb_ref[:, lanes] + cw_ref[n_taps - 1:n_taps, lanes] * cur
        for j in range(1, n_taps):
            xc = xc + cw_ref[n_taps - 1 - j:n_taps - j, lanes] * pltpu.roll(ext, j, axis=0)[SUBLANES:, :]
        return _silu(xc)

    bm = conv_silu(slice(d_ssd, d_ssd + SSD_GROUPS * gs)).astype(BF16)
    yield
    cm = conv_silu(slice(d_ssd + SSD_GROUPS * gs, cdim)).astype(BF16)
    yield
    cum, cum_t, dt_t, e_split, w_split = decay

    blk_lane = lax.broadcasted_iota(jnp.int32, (CHUNK, gw), 1) // SSD_HEAD_DIM
    for g in range(SSD_GROUPS):
        lanes = slice(g * gw, (g + 1) * gw)
        bg = bm[:, g * gs:(g + 1) * gs]
        cg = cm[:, g * gs:(g + 1) * gs]
        xs = conv_silu(lanes)
        xs_bf = xs.astype(BF16)
        ew_g = _dot(jnp.concatenate([e_split, w_split], axis=0), expand_ref[:, lanes])
        e_g = ew_g[:CHUNK, :]
        xw = (xs * ew_g[CHUNK:, :]).astype(BF16)
        cb_g = _dot_nt(cg, bg)
        yield
        m_parts, x_parts = [], []
        for j in range(hpg):
            h = g * hpg + j
            seg = (cum[:, h:h + 1] - cum_t[h:h + 1, :]) + negmask_ref[...]
            m_parts.append((cb_g * jnp.exp(seg) * dt_t[h:h + 1, :]).astype(BF16))
            x_parts.append(jnp.where(blk_lane == j, xs_bf, jnp.zeros_like(xs_bf)))
            if j % 2 == 1:
                yield
        y_diag = _dot(jnp.concatenate(m_parts, axis=1), jnp.concatenate(x_parts, axis=0))
        s_g = state[:, lanes]
        y_off = _dot(cg, s_g.astype(BF16))
        state[:, lanes] = s_g * e_g[CHUNK - 1:CHUNK, :] + _dot_tn(bg, xw)
        yg = y_diag + y_off * e_g + dsk_ref[:, lanes] * xs
        yg = yg * _silu(z[:, lanes].astype(F32))
        y[rows, lanes] = _rmsnorm(yg, nw_ref[:, lanes]).astype(y.dtype)
        yield

    tail[...] = xbc[CHUNK - TAIL:CHUNK, :]


def _gla_decays(sm, w2_ref, gb_ref, tri_ref, out, *, n_chunks, n_valid):
    dk = w2_ref.shape[-1]
    logit = _dot(sm[...].astype(BF16), w2_ref[...].astype(BF16)) + gb_ref[...]
    yield
    log_a = -_softplus(-logit) * (1.0 / GLA_GATE_TEMP)
    if n_valid is not None:
        row = lax.broadcasted_iota(jnp.int32, log_a.shape, 0)
        log_a = jnp.where(row < n_valid, log_a, 0.0)
    yield
    g_wide = _cumsum_rows(
        tri_ref[...], jnp.concatenate([log_a[c * CHUNK:(c + 1) * CHUNK, :] for c in range(n_chunks)], axis=1))
    out.extend(g_wide[:, c * dk:(c + 1) * dk] for c in range(n_chunks))
    yield


class _GlaChunk:
    def __init__(self, q, k, v, gg, g, nw_ref, tri01_ref, y, rows, row0, s_old, s_new, kf_scr, g_scr, *, n_valid):
        self.q, self.k, self.v, self.gg, self.g = q, k, v, gg, g
        self.nw_ref, self.tri01_ref = nw_ref, tri01_ref
        self.y, self.rows, self.row0 = y, rows, row0
        self.s_old, self.s_new = s_old, s_new
        self.kf_scr, self.g_scr = kf_scr, g_scr
        self.n_valid = n_valid
        self.hk = q.shape[-1] // GLA_HEADS
        self.hv = v.shape[-1] // GLA_HEADS
        self.g_mins = []

    def _lanes(self, h):
        return slice(h * self.hk, (h + 1) * self.hk), slice(h * self.hv, (h + 1) * self.hv)

    def _gate_decay(self, ks):
        kf = self.k[:, ks].astype(F32)
        if self.n_valid is not None:
            row = lax.broadcasted_iota(jnp.int32, (CHUNK, self.hk), 0) + self.row0
            kf = jnp.where(row < self.n_valid, kf, 0.0)
        qs = self.q[:, ks].astype(F32) * (self.hk ** -0.5)
        return qs, kf, self.g[:, ks]

    def _emit(self, h, q_dec, att):
        ks, vs = self._lanes(h)
        att = (att * self.tri01_ref[...]).astype(BF16)
        o = _dot(jnp.concatenate([q_dec, att], axis=1),
                 jnp.concatenate([self.s_old[:, vs].astype(BF16), self.v[:, vs]], axis=0))
        gate = _silu(self.gg[:, vs].astype(F32))
        self.y[self.rows, vs] = (_rmsnorm(o, self.nw_ref[...]) * gate).astype(self.y.dtype)

    def stages(self):
        for h in range(GLA_HEADS):
            ks, vs = self._lanes(h)
            qs, kf, g = self._gate_decay(ks)
            yield
            g_last = g[CHUNK - 1:CHUNK, :]
            q_dec = (qs * jnp.exp(g)).astype(BF16)
            k_inv = (kf * jnp.exp(jnp.minimum(-g, -GLA_SAFE_LOG_DECAY))).astype(BF16)
            k_dec = (kf * jnp.exp(g_last - g)).astype(BF16)
            att = _dot_nt(q_dec, k_inv)
            self.g_mins.append(jnp.min(g_last))
            yield
            self._emit(h, q_dec, att)
            decay_col = jnp.broadcast_to(jnp.exp(g_last), (CHUNK, self.hk)).T
            decay_col = jnp.concatenate([decay_col] * (self.hv // self.hk), axis=1)
            self.s_new[:, vs] = self.s_old[:, vs] * decay_col + _dot_tn(k_dec, self.v[:, vs])
            yield

    def redo_unsafe(self):
        g_min = functools.reduce(jnp.minimum, self.g_mins)

        @pl.when(g_min < GLA_SAFE_LOG_DECAY)
        def _():
            col_id = lax.broadcasted_iota(jnp.int32, (CHUNK, CHUNK), 1)
            for h in range(GLA_HEADS):
                ks, _ = self._lanes(h)
                q_h, kf, g_h = self._gate_decay(ks)
                self.kf_scr[...] = kf
                self.g_scr[...] = g_h

                def key_rows(sb, att, q_h=q_h, g_h=g_h):
                    s0 = pl.multiple_of(sb * SUBLANES, SUBLANES)
                    k_rows = self.kf_scr[pl.ds(s0, SUBLANES), :]
                    g_rows = self.g_scr[pl.ds(s0, SUBLANES), :]
                    for j in range(SUBLANES):
                        p = q_h * k_rows[j:j + 1, :] * jnp.exp(jnp.minimum(g_h - g_rows[j:j + 1, :], 0.0))
                        att = jnp.where(col_id == s0 + j, jnp.sum(p, axis=1, keepdims=True), att)
                    return att

                att = lax.fori_loop(0, CHUNK // SUBLANES, key_rows, jnp.zeros((CHUNK, CHUNK), F32))
                self._emit(h, (q_h * jnp.exp(g_h)).astype(BF16), att)


def _mixer_body(h_ref, hn_ref, nw_ref, wm_ref, ws_ref,
                cw_ref, cb_ref, dtb_ref, alog_ref, dsk_ref, snw_ref, s0_ref, t0_ref,
                w2_ref, gb_ref, gnw_ref, g0_ref,
                ys_ref, yg_ref, mg_ref, sf_ref, tf_ref, gf_ref,
                u_s, z_s, xbc_s, q_s, k_s, v_s, gg_s, sm_s, ssd_state, tail, gla_state, kf_scr, g_scr,
                expand_s, tri_s, tri01_s, negmask_s,
                *, tile, n_valid, tiles_per_seq):
    i = pl.program_id(0)
    n_chunks = tile // CHUNK
    d = h_ref.shape[-1]
    heads = d // SSD_HEAD_DIM
    cur = i % 2
    nxt = 1 - cur

    slots = (("z", z_s), ("xbc", xbc_s), ("q", q_s), ("k", k_s), ("v", v_s), ("gg", gg_s))
    off = {}
    start = 0
    for name, ref in slots:
        off[name] = start
        start += ref.shape[-1]
    off_mg = start

    def normalise_jobs(src_ref, slot):
        def job(r0):
            def run():
                u_s[slot, r0:r0 + CHUNK, :] = _rmsnorm(src_ref[r0:r0 + CHUNK, :], nw_ref[...]).astype(BF16)
            return run
        return [job(r0) for r0 in range(0, tile, CHUNK)]

    def project_jobs(slot):
        def small():
            sm_s[slot] = _dot(u_s[slot], ws_ref[...])
        jobs = [small]
        for name in ("xbc", "z", "q", "k", "v", "gg"):
            ref = dict(slots)[name]
            width = ref.shape[-1]
            step = min(width, PROJ_LANES)
            for c0 in range(0, width, step):
                def job(ref=ref, c0=c0, step=step, o=off[name]):
                    ref[slot, :, c0:c0 + step] = _dot(u_s[slot], wm_ref[:, o + c0:o + c0 + step]).astype(ref.dtype)
                jobs.append(job)
        return jobs

    def merge_gate_jobs(slot):
        jobs = []
        for c0 in range(0, mg_ref.shape[-1], PROJ_LANES):
            def job(c0=c0):
                mg_ref[:, c0:c0 + PROJ_LANES] = _dot(
                    u_s[slot], wm_ref[:, off_mg + c0:off_mg + c0 + PROJ_LANES]).astype(mg_ref.dtype)
            jobs.append(job)
        return jobs

    @pl.when(i == 0)
    def _():
        er = lax.broadcasted_iota(jnp.int32, expand_s.shape, 0)
        ec = lax.broadcasted_iota(jnp.int32, expand_s.shape, 1) // SSD_HEAD_DIM
        expand_s[...] = ((er == ec) | (er == ec + 2 * heads)).astype(BF16)
        tr = lax.broadcasted_iota(jnp.int32, (CHUNK, CHUNK), 0)
        tc = lax.broadcasted_iota(jnp.int32, (CHUNK, CHUNK), 1)
        tri_s[...] = (tr >= tc).astype(BF16)
        tri01_s[...] = (tr >= tc).astype(F32)
        negmask_s[...] = jnp.where(tr >= tc, 0.0, NEG_BIG)
        for job in normalise_jobs(h_ref, 0) + project_jobs(0):
            job()

    @pl.when(i % tiles_per_seq == 0)
    def _():
        ssd_state[...] = s0_ref[...]
        tail[...] = t0_ref[...]
        gla_state[0] = g0_ref[...]

    ssd_prm = (cw_ref, cb_ref, dsk_ref, snw_ref)
    ssd_consts = (expand_s, negmask_s)
    nv = None if n_valid == tile else n_valid

    fillers = normalise_jobs(hn_ref, nxt) + project_jobs(nxt) + merge_gate_jobs(cur)
    per_chunk = -(-len(fillers) // n_chunks)
    ssd_decays, gla_decays = [], []
    _round_robin(
        [_ssd_decays(sm_s.at[cur], dtb_ref, alog_ref, tri_s, ssd_decays, n_chunks=n_chunks, heads=heads, n_valid=nv),
         _gla_decays(sm_s.at[cur], w2_ref, gb_ref, tri_s, gla_decays, n_chunks=n_chunks, n_valid=nv)],
        fillers, 0, 1)
    glas = []
    for c in range(n_chunks):
        r0 = c * CHUNK
        rows = slice(r0, r0 + CHUNK)
        view = lambda ref: ref.at[cur, rows]
        gla = _GlaChunk(view(q_s), view(k_s), view(v_s), view(gg_s), gla_decays[c], gnw_ref, tri01_s,
                        yg_ref, rows, r0, gla_state.at[c], gla_state.at[c + 1], kf_scr, g_scr, n_valid=nv)
        ssd = _ssd_chunk(view(xbc_s), view(z_s), ssd_decays[c], ssd_prm, ssd_consts, ys_ref, rows, ssd_state, tail)
        _round_robin([ssd, gla.stages()], fillers, per_chunk, FILL_STRIDE)
        glas.append(gla)
    assert not fillers

    for gla in glas:
        gla.redo_unsafe()
    gla_state[0] = gla_state[n_chunks]

    if sf_ref is not None:
        @pl.when(i % tiles_per_seq == tiles_per_seq - 1)
        def _():
            sf_ref[0] = ssd_state[...]
            tf_ref[0] = xbc_s[cur, n_valid - TAIL:n_valid, :]
            gf_ref[0] = gla_state[n_chunks]


def _mixer_body_no_state(*refs, **kw):
    n_in, n_out = 17, 3
    ins, outs, scratch = refs[:n_in], refs[n_in:n_in + n_out], refs[n_in + n_out:]
    _mixer_body(*ins, *outs, None, None, None, *scratch, **kw)


def _mixer(h, p, states, tile, n_valid, emit_state):
    b, l, d = h.shape
    cdim = p["conv_w"].shape[-1]
    assert p["conv_w"].shape[0] - 1 <= SUBLANES
    dk = p["gate_b"].shape[-1]
    dv = d
    hk = dk // GLA_HEADS
    n_main = p["w_main"].shape[1]
    tiles_per_seq = l // tile
    n_tiles = b * tiles_per_seq
    n_chunks = tile // CHUNK
    s0, t0, g0 = states
    h2d = h.reshape(b * l, d)
    tok = lambda i: (i, 0)
    nxt = lambda i: (jnp.minimum(i + 1, n_tiles - 1), 0)
    const = lambda i: (0, 0)
    per_b = lambda i: (i // tiles_per_seq, 0, 0)
    whole = lambda shape: pl.BlockSpec(shape, const)
    resident = lambda shape: pl.BlockSpec(shape, const, pipeline_mode=pl.Buffered(1))
    out_shape = [jax.ShapeDtypeStruct((b * l, d), BF16),
                 jax.ShapeDtypeStruct((b * l, dv), BF16),
                 jax.ShapeDtypeStruct((b * l, 2 * d), BF16)]
    out_specs = [pl.BlockSpec((tile, d), tok), pl.BlockSpec((tile, dv), tok), pl.BlockSpec((tile, 2 * d), tok)]
    if emit_state:
        out_shape += [jax.ShapeDtypeStruct((b, SSD_STATE, d), F32),
                      jax.ShapeDtypeStruct((b, TAIL, cdim), BF16),
                      jax.ShapeDtypeStruct((b, hk, dv), F32)]
        out_specs += [pl.BlockSpec((1, SSD_STATE, d), per_b),
                      pl.BlockSpec((1, TAIL, cdim), per_b),
                      pl.BlockSpec((1, hk, dv), per_b)]
    kw = dict(tile=tile, n_valid=n_valid, tiles_per_seq=tiles_per_seq)
    body = functools.partial(_mixer_body if emit_state else _mixer_body_no_state, **kw)
    return pl.pallas_call(
        body,
        out_shape=out_shape,
        grid=(n_tiles,),
        in_specs=[
            resident((tile, d)),
            pl.BlockSpec((tile, d), nxt),
            whole((1, d)),
            resident((d, n_main)),
            resident((d, SMALL_W)),
            whole(p["conv_w"].shape), whole((1, cdim)), whole((1, SMALL_W)), whole((1, SMALL_W)),
            whole((1, d)), whole((1, d)), whole((SSD_STATE, d)), whole((TAIL, cdim)),
            whole((SMALL_W, dk)), whole((1, dk)), whole((1, dv // GLA_HEADS)), whole((hk, dv)),
        ],
        out_specs=out_specs,
        scratch_shapes=[pltpu.VMEM((2, tile, d), BF16),
                        pltpu.VMEM((2, tile, d), BF16),
                        pltpu.VMEM((2, tile, cdim), BF16),
                        pltpu.VMEM((2, tile, dk), BF16),
                        pltpu.VMEM((2, tile, dk), BF16),
                        pltpu.VMEM((2, tile, dv), BF16),
                        pltpu.VMEM((2, tile, dv), BF16),
                        pltpu.VMEM((2, tile, SMALL_W), F32),
                        pltpu.VMEM((SSD_STATE, d), F32),
                        pltpu.VMEM((TAIL, cdim), BF16),
                        pltpu.VMEM((n_chunks + 1, hk, dv), F32),
                        pltpu.VMEM((CHUNK, hk), F32),
                        pltpu.VMEM((CHUNK, hk), F32),
                        pltpu.VMEM((SMALL_W, d), BF16),
                        pltpu.VMEM((CHUNK, CHUNK), BF16),
                        pltpu.VMEM((CHUNK, CHUNK), F32),
                        pltpu.VMEM((CHUNK, CHUNK), F32)],
        compiler_params=pltpu.CompilerParams(
            dimension_semantics=("arbitrary",), vmem_limit_bytes=VMEM_LIMIT),
        name="mixer",
    )(h2d, h2d, p["norm_w"], p["w_main"], p["w_small"],
      p["conv_w"], p["conv_b"], p["dt_bias"], p["a_log"], p["d_skip"], p["ssd_norm_w"], s0, t0,
      p["gate_w2"], p["gate_b"], p["gla_norm_w"], g0)


def _merge_body(ys_ref, yg_ref, mg_ref, h_ref, wb_ref, wo_ref, fnw_ref, o_ref, *, final_norm):
    d = h_ref.shape[-1]
    bp0 = _dot(ys_ref[...], wb_ref[0])
    bp1 = _dot(yg_ref[...], wb_ref[1])
    g0 = _sigmoid(mg_ref[:, :d].astype(F32))
    g1 = _sigmoid(mg_ref[:, d:].astype(F32))
    merged = (g0 * bp0 + g1 * bp1).astype(BF16)
    out = h_ref[...] + _dot(merged, wo_ref[...])
    if final_norm:
        out = _rmsnorm(out, fnw_ref[...])
    o_ref[...] = out


def _merge(ys, yg, mg, h2d, w_branch, w_out, final_norm_w, tm, final_norm):
    m, d = h2d.shape
    row = lambda i: (i, 0)
    return pl.pallas_call(
        functools.partial(_merge_body, final_norm=final_norm),
        out_shape=jax.ShapeDtypeStruct((m, d), F32),
        grid=(m // tm,),
        in_specs=[
            pl.BlockSpec((tm, d), row),
            pl.BlockSpec((tm, d), row),
            pl.BlockSpec((tm, 2 * d), row),
            pl.BlockSpec((tm, d), row),
            pl.BlockSpec(w_branch.shape, lambda i: (0, 0, 0)),
            pl.BlockSpec(w_out.shape, lambda i: (0, 0)),
            pl.BlockSpec((1, d), lambda i: (0, 0)),
        ],
        out_specs=pl.BlockSpec((tm, d), row),
        compiler_params=pltpu.CompilerParams(
            dimension_semantics=("arbitrary",), vmem_limit_bytes=VMEM_LIMIT),
        name="merge",
    )(ys, yg, mg, h2d, w_branch, w_out, final_norm_w)


def _pad_lanes(v, width):
    v = v.reshape(1, -1)
    return jnp.pad(v, ((0, 0), (0, width - v.shape[1])))


def _layer_params(i, norm_w, w_in, conv_w, conv_b, dt_bias, a_log, d_skip, ssd_norm_w,
                  gla_gate_w2, gla_gate_b, gla_norm_w, w_branch, w_out):
    d = w_out.shape[-1]
    heads = dt_bias.shape[-1]
    rank = gla_gate_w2.shape[1]
    dk = gla_gate_w2.shape[2]
    dv = d
    cdim = conv_w.shape[-1]
    widths = (d, cdim, heads, dk, dk, dv, dv, rank, 2 * d)
    starts = [0]
    for w in widths:
        starts.append(starts[-1] + w)
    col = lambda j: w_in[i][:, starts[j]:starts[j + 1]]
    w_main = jnp.concatenate([col(0), col(1), col(3), col(4), col(5), col(6), col(8)], axis=1).astype(BF16)
    assert rank == heads and 3 * heads <= SMALL_W
    w_small = jnp.concatenate(
        [col(2), col(7), col(2), jnp.zeros((d, SMALL_W - 3 * heads), F32)], axis=1).astype(BF16)
    dtb = jnp.concatenate([dt_bias[i], jnp.zeros((heads,), F32), dt_bias[i]])
    alog = jnp.concatenate([a_log[i], jnp.zeros((heads,), F32), a_log[i]])
    w2 = jnp.zeros((SMALL_W, dk), F32).at[heads:heads + rank].set(gla_gate_w2[i])
    return dict(
        norm_w=norm_w[i].reshape(1, d), w_main=w_main, w_small=w_small,
        conv_w=conv_w[i], conv_b=conv_b[i].reshape(1, cdim),
        dt_bias=_pad_lanes(dtb, SMALL_W), a_log=_pad_lanes(alog, SMALL_W),
        d_skip=jnp.repeat(d_skip[i], SSD_HEAD_DIM).reshape(1, d),
        ssd_norm_w=ssd_norm_w[i].reshape(1, d),
        gate_w2=w2, gate_b=gla_gate_b[i].reshape(1, dk), gla_norm_w=gla_norm_w[i].reshape(1, -1),
        w_branch=w_branch[i].astype(BF16), w_out=w_out[i].astype(BF16))


def _layer(h, p, states, tile, tm, n_valid, final_norm_w, final_norm, emit_state):
    b, l, d = h.shape
    outs = _mixer(h, p, states, tile, n_valid, emit_state)
    y_ssd, y_gla, mg = outs[:3]
    out = _merge(y_ssd, y_gla, mg, h.reshape(b * l, d), p["w_branch"], p["w_out"], final_norm_w, tm, final_norm)
    final = (outs[3][0], outs[4][0], outs[5][0]) if emit_state else None
    return out.reshape(b, l, d), final


def kernel(x, meta_tokens, norm_w, w_in, conv_w, conv_b, dt_bias, a_log, d_skip, ssd_norm_w, gla_gate_w2,
           gla_gate_b, gla_norm_w, w_branch, w_out, final_norm_w):
    b, seq, d = x.shape
    depth = norm_w.shape[0]
    n_meta = meta_tokens.shape[0]
    assert TAIL <= n_meta <= CHUNK and n_meta % TAIL == 0 and seq % CHUNK == 0
    tile = 512 if seq % 512 == 0 else 2 * CHUNK
    tm = 512 if (b * seq) % 512 == 0 else CHUNK
    cdim = conv_w.shape[-1]
    dk = gla_gate_w2.shape[2]
    fnw = final_norm_w.reshape(1, d)
    zero_states = (jnp.zeros((SSD_STATE, d), F32), jnp.zeros((TAIL, cdim), BF16),
                   jnp.zeros((dk // GLA_HEADS, d), F32))
    h_meta = jnp.pad(meta_tokens.astype(x.dtype), ((0, CHUNK - n_meta), (0, 0)))[None]
    h = x
    for i in range(depth):
        p = _layer_params(i, norm_w, w_in, conv_w, conv_b, dt_bias, a_log, d_skip, ssd_norm_w,
                          gla_gate_w2, gla_gate_b, gla_norm_w, w_branch, w_out)
        h_meta, seeded = _layer(h_meta, p, zero_states, CHUNK, CHUNK, n_meta, fnw, False, True)
        h, _ = _layer(h, p, seeded, tile, tm, tile, fnw, i == depth - 1, False)
    return h
```

```python
import functools

import jax
import jax.numpy as jnp
from jax import lax
from jax.experimental import pallas as pl
from jax.experimental.pallas import tpu as pltpu

F32 = jnp.float32
BF16 = jnp.bfloat16

SSD_HEAD_DIM = 64
SSD_GROUPS = 4
SSD_STATE = 128
GLA_HEADS = 4
GLA_GATE_TEMP = 16.0
EPS = 1e-6

CHUNK = 128
SMALL_W = 128
SUBLANES = 8
TAIL = 16
PROJ_LANES = 512
NEG_BIG = -1e30
GLA_SAFE_LOG_DECAY = -60.0
FILL_STRIDE = 2
VMEM_LIMIT = 60 * 1024 * 1024


def _sigmoid(x):
    return 0.5 * (jnp.tanh(0.5 * x) + 1.0)


def _silu(x):
    h = 0.5 * x
    return h + h * jnp.tanh(h)


def _softplus(x):
    return jnp.maximum(x, 0.0) + jnp.log(1.0 + jnp.exp(-jnp.abs(x)))


def _rmsnorm(x, w):
    return x * lax.rsqrt(jnp.mean(x * x, axis=-1, keepdims=True) + EPS) * w


def _split2(x):
    hi = x.astype(BF16)
    mid = (x - hi.astype(F32)).astype(BF16)
    return hi, mid


def _dot(a, b):
    return jnp.dot(a, b, preferred_element_type=F32)


def _dot_nt(a, b):
    return lax.dot_general(a, b, (((1,), (1,)), ((), ())), preferred_element_type=F32)


def _dot_tn(a, b):
    return lax.dot_general(a, b, (((0,), (0,)), ((), ())), preferred_element_type=F32)


def _cumsum_rows(tri_bf, x):
    hi, mid = _split2(x)
    return _dot(tri_bf, hi) + _dot(tri_bf, mid)


def _round_robin(generators, fillers, n_fill, stride):
    generators = list(generators)
    n_fill = min(n_fill, len(fillers))
    rnd = 0
    while generators or n_fill:
        for gen in list(generators):
            if next(gen, StopIteration) is StopIteration:
                generators.remove(gen)
        if n_fill and (rnd % stride == 0 or not generators):
            fillers.pop(0)()
            n_fill -= 1
        rnd += 1


def _ssd_decays(sm, dtb_ref, alog_ref, tri_ref, out, *, n_chunks, heads, n_valid):
    head_lane = lax.broadcasted_iota(jnp.int32, (CHUNK, SMALL_W), 1) < heads
    a_full = -jnp.exp(alog_ref[...])
    dts = []
    for c in range(n_chunks):
        dt = _softplus(sm[c * CHUNK:(c + 1) * CHUNK, :] + dtb_ref[...])
        dt = jnp.where(head_lane, dt, 0.0)
        if n_valid is not None:
            row = lax.broadcasted_iota(jnp.int32, (CHUNK, SMALL_W), 0) + c * CHUNK
            dt = jnp.where(row < n_valid, dt, 0.0)
        dts.append(dt)
    yield
    cum_wide = _cumsum_rows(tri_ref[...], jnp.concatenate([dt * a_full for dt in dts], axis=1))
    yield

    for c, dt in enumerate(dts):
        cum = cum_wide[:, c * SMALL_W:(c + 1) * SMALL_W]
        cum_last = cum[CHUNK - 1:CHUNK, :]
        out.append((cum, cum.T, dt.T, jnp.exp(cum), jnp.exp(cum_last - cum) * dt))
    yield


def _ssd_chunk(xbc, z, decay, prm, negmask_ref, y, rows, state, tail):
    cw_ref, cb_ref, dsk_ref, nw_ref = prm
    d_ssd = z.shape[-1]
    cdim = xbc.shape[-1]
    heads = d_ssd // SSD_HEAD_DIM
    gw = d_ssd // SSD_GROUPS
    hpg = heads // SSD_GROUPS
    gs = SSD_STATE
    n_taps = cw_ref.shape[0]

    def conv_silu(lanes):
        cur = xbc[:, lanes].astype(F32)
        hist = tail[:, lanes].astype(F32)[TAIL - SUBLANES:, :]
        ext = jnp.concatenate([hist, cur], axis=0)
        xc = cb_ref[:, lanes] + cw_ref[n_taps - 1:n_taps, lanes] * cur
        for j in range(1, n_taps):
            xc = xc + cw_ref[n_taps - 1 - j:n_taps - j, lanes] * pltpu.roll(ext, j, axis=0)[SUBLANES:, :]
        return _silu(xc)

    bm = conv_silu(slice(d_ssd, d_ssd + SSD_GROUPS * gs)).astype(BF16)
    yield
    cm = conv_silu(slice(d_ssd + SSD_GROUPS * gs, cdim)).astype(BF16)
    yield
    cum, cum_t, dt_t, e_split, w_split = decay

    blk_lane = lax.broadcasted_iota(jnp.int32, (CHUNK, gw), 1) // SSD_HEAD_DIM
    for g in range(SSD_GROUPS):
        lanes = slice(g * gw, (g + 1) * gw)
        bg = bm[:, g * gs:(g + 1) * gs]
        cg = cm[:, g * gs:(g + 1) * gs]
        xs = conv_silu(lanes)
        xs_bf = xs.astype(BF16)
        def spread(v):
            wide = jnp.broadcast_to(v[:, g * hpg:g * hpg + 1], (CHUNK, gw))
            for j in range(1, hpg):
                wide = jnp.where(blk_lane == j, jnp.broadcast_to(v[:, g * hpg + j:g * hpg + j + 1], (CHUNK, gw)), wide)
            return wide

        e_g = spread(e_split)
        xw = (xs * spread(w_split)).astype(BF16)
        cb_g = _dot_nt(cg, bg)
        yield
        m_parts, x_parts = [], []
        for j in range(hpg):
            h = g * hpg + j
            seg = (cum[:, h:h + 1] - cum_t[h:h + 1, :]) + negmask_ref[...]
            m_parts.append((cb_g * jnp.exp(seg) * dt_t[h:h + 1, :]).astype(BF16))
            x_parts.append(jnp.where(blk_lane == j, xs_bf, jnp.zeros_like(xs_bf)))
            if j % 2 == 1:
                yield
        y_diag = _dot(jnp.concatenate(m_parts, axis=1), jnp.concatenate(x_parts, axis=0))
        s_g = state[:, lanes]
        y_off = _dot(cg, s_g.astype(BF16))
        state[:, lanes] = s_g * e_g[CHUNK - 1:CHUNK, :] + _dot_tn(bg, xw)
        yg = y_diag + y_off * e_g + dsk_ref[:, lanes] * xs
        yg = yg * _silu(z[:, lanes].astype(F32))
        y[rows, lanes] = _rmsnorm(yg, nw_ref[:, lanes]).astype(y.dtype)
        yield

    tail[...] = xbc[CHUNK - TAIL:CHUNK, :]


def _gla_decays(sm, w2_ref, gb_ref, tri_ref, out, *, n_chunks, n_valid):
    dk = w2_ref.shape[-1]
    logit = _dot(sm[...].astype(BF16), w2_ref[...].astype(BF16)) + gb_ref[...]
    yield
    log_a = -_softplus(-logit) * (1.0 / GLA_GATE_TEMP)
    if n_valid is not None:
        row = lax.broadcasted_iota(jnp.int32, log_a.shape, 0)
        log_a = jnp.where(row < n_valid, log_a, 0.0)
    yield
    g_wide = _cumsum_rows(
        tri_ref[...], jnp.concatenate([log_a[c * CHUNK:(c + 1) * CHUNK, :] for c in range(n_chunks)], axis=1))
    out.extend(g_wide[:, c * dk:(c + 1) * dk] for c in range(n_chunks))
    yield


class _GlaChunk:
    def __init__(self, q, k, v, gg, g, nw_ref, tri01_ref, y, rows, row0, s_old, s_new, kf_scr, g_scr, *, n_valid):
        self.q, self.k, self.v, self.gg, self.g = q, k, v, gg, g
        self.nw_ref, self.tri01_ref = nw_ref, tri01_ref
        self.y, self.rows, self.row0 = y, rows, row0
        self.s_old, self.s_new = s_old, s_new
        self.kf_scr, self.g_scr = kf_scr, g_scr
        self.n_valid = n_valid
        self.hk = q.shape[-1] // GLA_HEADS
        self.hv = v.shape[-1] // GLA_HEADS
        self.g_mins = []

    def _lanes(self, h):
        return slice(h * self.hk, (h + 1) * self.hk), slice(h * self.hv, (h + 1) * self.hv)

    def _gate_decay(self, ks):
        kf = self.k[:, ks].astype(F32)
        if self.n_valid is not None:
            row = lax.broadcasted_iota(jnp.int32, (CHUNK, self.hk), 0) + self.row0
            kf = jnp.where(row < self.n_valid, kf, 0.0)
        qs = self.q[:, ks].astype(F32) * (self.hk ** -0.5)
        return qs, kf, self.g[:, ks]

    def _emit(self, h, q_dec, att):
        ks, vs = self._lanes(h)
        att = (att * self.tri01_ref[...]).astype(BF16)
        o = _dot(jnp.concatenate([q_dec, att], axis=1),
                 jnp.concatenate([self.s_old[:, vs].astype(BF16), self.v[:, vs]], axis=0))
        gate = _silu(self.gg[:, vs].astype(F32))
        self.y[self.rows, vs] = (_rmsnorm(o, self.nw_ref[...]) * gate).astype(self.y.dtype)

    def stages(self):
        for h in range(GLA_HEADS):
            ks, vs = self._lanes(h)
            qs, kf, g = self._gate_decay(ks)
            yield
            g_last = g[CHUNK - 1:CHUNK, :]
            q_dec = (qs * jnp.exp(g)).astype(BF16)
            k_inv = (kf * jnp.exp(jnp.minimum(-g, -GLA_SAFE_LOG_DECAY))).astype(BF16)
            k_dec = (kf * jnp.exp(g_last - g)).astype(BF16)
            att = _dot_nt(q_dec, k_inv)
            self.g_mins.append(jnp.min(g_last))
            yield
            self._emit(h, q_dec, att)
            decay_col = jnp.broadcast_to(jnp.exp(g_last), (CHUNK, self.hk)).T
            decay_col = jnp.concatenate([decay_col] * (self.hv // self.hk), axis=1)
            self.s_new[:, vs] = self.s_old[:, vs] * decay_col + _dot_tn(k_dec, self.v[:, vs])
            yield

    def redo_unsafe(self):
        g_min = functools.reduce(jnp.minimum, self.g_mins)

        @pl.when(g_min < GLA_SAFE_LOG_DECAY)
        def _():
            col_id = lax.broadcasted_iota(jnp.int32, (CHUNK, CHUNK), 1)
            for h in range(GLA_HEADS):
                ks, _ = self._lanes(h)
                q_h, kf, g_h = self._gate_decay(ks)
                self.kf_scr[...] = kf
                self.g_scr[...] = g_h

                def key_rows(sb, att, q_h=q_h, g_h=g_h):
                    s0 = pl.multiple_of(sb * SUBLANES, SUBLANES)
                    k_rows = self.kf_scr[pl.ds(s0, SUBLANES), :]
                    g_rows = self.g_scr[pl.ds(s0, SUBLANES), :]
                    for j in range(SUBLANES):
                        p = q_h * k_rows[j:j + 1, :] * jnp.exp(jnp.minimum(g_h - g_rows[j:j + 1, :], 0.0))
                        att = jnp.where(col_id == s0 + j, jnp.sum(p, axis=1, keepdims=True), att)
                    return att

                att = lax.fori_loop(0, CHUNK // SUBLANES, key_rows, jnp.zeros((CHUNK, CHUNK), F32))
                self._emit(h, (q_h * jnp.exp(g_h)).astype(BF16), att)


def _mixer_body(h_ref, hn_ref, nw_ref, wm_ref, ws_ref,
                cw_ref, cb_ref, dtb_ref, alog_ref, dsk_ref, snw_ref, s0_ref, t0_ref,
                w2_ref, gb_ref, gnw_ref, g0_ref,
                ys_ref, yg_ref, mg_ref, sf_ref, tf_ref, gf_ref,
                u_s, z_s, xbc_s, q_s, k_s, v_s, gg_s, sm_s, ssd_state, tail, gla_state, kf_scr, g_scr,
                tri_s, tri01_s, negmask_s,
                *, tile, n_valid, tiles_per_seq):
    i = pl.program_id(0)
    n_chunks = tile // CHUNK
    d = h_ref.shape[-1]
    heads = d // SSD_HEAD_DIM
    cur = i % 2
    nxt = 1 - cur

    slots = (("z", z_s), ("xbc", xbc_s), ("q", q_s), ("k", k_s), ("v", v_s), ("gg", gg_s))
    off = {}
    start = 0
    for name, ref in slots:
        off[name] = start
        start += ref.shape[-1]
    off_mg = start

    def normalise_jobs(src_ref, slot):
        def job(r0):
            def run():
                u_s[slot, r0:r0 + CHUNK, :] = _rmsnorm(src_ref[r0:r0 + CHUNK, :], nw_ref[...]).astype(BF16)
            return run
        return [job(r0) for r0 in range(0, tile, CHUNK)]

    def project_jobs(slot):
        def small():
            sm_s[slot] = _dot(u_s[slot], ws_ref[...])
        jobs = [small]
        for name in ("xbc", "z", "q", "k", "v", "gg"):
            ref = dict(slots)[name]
            width = ref.shape[-1]
            step = min(width, PROJ_LANES)
            for c0 in range(0, width, step):
                def job(ref=ref, c0=c0, step=step, o=off[name]):
                    ref[slot, :, c0:c0 + step] = _dot(u_s[slot], wm_ref[:, o + c0:o + c0 + step]).astype(ref.dtype)
                jobs.append(job)
        return jobs

    def merge_gate_jobs(slot):
        jobs = []
        for c0 in range(0, mg_ref.shape[-1], PROJ_LANES):
            def job(c0=c0):
                mg_ref[:, c0:c0 + PROJ_LANES] = _dot(
                    u_s[slot], wm_ref[:, off_mg + c0:off_mg + c0 + PROJ_LANES]).astype(mg_ref.dtype)
            jobs.append(job)
        return jobs

    @pl.when(i == 0)
    def _():
        tr = lax.broadcasted_iota(jnp.int32, (CHUNK, CHUNK), 0)
        tc = lax.broadcasted_iota(jnp.int32, (CHUNK, CHUNK), 1)
        tri_s[...] = (tr >= tc).astype(BF16)
        tri01_s[...] = (tr >= tc).astype(F32)
        negmask_s[...] = jnp.where(tr >= tc, 0.0, NEG_BIG)
        for job in normalise_jobs(h_ref, 0) + project_jobs(0):
            job()

    @pl.when(i % tiles_per_seq == 0)
    def _():
        ssd_state[...] = s0_ref[...]
        tail[...] = t0_ref[...]
        gla_state[0] = g0_ref[...]

    ssd_prm = (cw_ref, cb_ref, dsk_ref, snw_ref)
    nv = None if n_valid == tile else n_valid

    fillers = normalise_jobs(hn_ref, nxt) + project_jobs(nxt) + merge_gate_jobs(cur)
    per_chunk = -(-len(fillers) // n_chunks)
    ssd_decays, gla_decays = [], []
    _round_robin(
        [_ssd_decays(sm_s.at[cur], dtb_ref, alog_ref, tri_s, ssd_decays, n_chunks=n_chunks, heads=heads, n_valid=nv),
         _gla_decays(sm_s.at[cur], w2_ref, gb_ref, tri_s, gla_decays, n_chunks=n_chunks, n_valid=nv)],
        fillers, 0, 1)
    glas = []
    for c in range(n_chunks):
        r0 = c * CHUNK
        rows = slice(r0, r0 + CHUNK)
        view = lambda ref: ref.at[cur, rows]
        gla = _GlaChunk(view(q_s), view(k_s), view(v_s), view(gg_s), gla_decays[c], gnw_ref, tri01_s,
                        yg_ref, rows, r0, gla_state.at[c], gla_state.at[c + 1], kf_scr, g_scr, n_valid=nv)
        ssd = _ssd_chunk(view(xbc_s), view(z_s), ssd_decays[c], ssd_prm, negmask_s, ys_ref, rows, ssd_state, tail)
        _round_robin([ssd, gla.stages()], fillers, per_chunk, FILL_STRIDE)
        glas.append(gla)
    assert not fillers

    for gla in glas:
        gla.redo_unsafe()
    gla_state[0] = gla_state[n_chunks]

    if sf_ref is not None:
        @pl.when(i % tiles_per_seq == tiles_per_seq - 1)
        def _():
            sf_ref[0] = ssd_state[...]
            tf_ref[0] = xbc_s[cur, n_valid - TAIL:n_valid, :]
            gf_ref[0] = gla_state[n_chunks]


def _mixer_body_no_state(*refs, **kw):
    n_in, n_out = 17, 3
    ins, outs, scratch = refs[:n_in], refs[n_in:n_in + n_out], refs[n_in + n_out:]
    _mixer_body(*ins, *outs, None, None, None, *scratch, **kw)


def _mixer(h, p, states, tile, n_valid, emit_state):
    b, l, d = h.shape
    cdim = p["conv_w"].shape[-1]
    assert p["conv_w"].shape[0] - 1 <= SUBLANES
    dk = p["gate_b"].shape[-1]
    dv = d
    hk = dk // GLA_HEADS
    n_main = p["w_main"].shape[1]
    tiles_per_seq = l // tile
    n_tiles = b * tiles_per_seq
    n_chunks = tile // CHUNK
    s0, t0, g0 = states
    h2d = h.reshape(b * l, d)
    tok = lambda i: (i, 0)
    nxt = lambda i: (jnp.minimum(i + 1, n_tiles - 1), 0)
    const = lambda i: (0, 0)
    per_b = lambda i: (i // tiles_per_seq, 0, 0)
    whole = lambda shape: pl.BlockSpec(shape, const)
    resident = lambda shape: pl.BlockSpec(shape, const, pipeline_mode=pl.Buffered(1))
    out_shape = [jax.ShapeDtypeStruct((b * l, d), BF16),
                 jax.ShapeDtypeStruct((b * l, dv), BF16),
                 jax.ShapeDtypeStruct((b * l, 2 * d), BF16)]
    out_specs = [pl.BlockSpec((tile, d), tok), pl.BlockSpec((tile, dv), tok), pl.BlockSpec((tile, 2 * d), tok)]
    if emit_state:
        out_shape += [jax.ShapeDtypeStruct((b, SSD_STATE, d), F32),
                      jax.ShapeDtypeStruct((b, TAIL, cdim), BF16),
                      jax.ShapeDtypeStruct((b, hk, dv), F32)]
        out_specs += [pl.BlockSpec((1, SSD_STATE, d), per_b),
                      pl.BlockSpec((1, TAIL, cdim), per_b),
                      pl.BlockSpec((1, hk, dv), per_b)]
    kw = dict(tile=tile, n_valid=n_valid, tiles_per_seq=tiles_per_seq)
    body = functools.partial(_mixer_body if emit_state else _mixer_body_no_state, **kw)
    return pl.pallas_call(
        body,
        out_shape=out_shape,
        grid=(n_tiles,),
        in_specs=[
            resident((tile, d)),
            pl.BlockSpec((tile, d), nxt),
            whole((1, d)),
            resident((d, n_main)),
            resident((d, SMALL_W)),
            whole(p["conv_w"].shape), whole((1, cdim)), whole((1, SMALL_W)), whole((1, SMALL_W)),
            whole((1, d)), whole((1, d)), whole((SSD_STATE, d)), whole((TAIL, cdim)),
            whole((SMALL_W, dk)), whole((1, dk)), whole((1, dv // GLA_HEADS)), whole((hk, dv)),
        ],
        out_specs=out_specs,
        scratch_shapes=[pltpu.VMEM((2, tile, d), BF16),
                        pltpu.VMEM((2, tile, d), BF16),
                        pltpu.VMEM((2, tile, cdim), BF16),
                        pltpu.VMEM((2, tile, dk), BF16),
                        pltpu.VMEM((2, tile, dk), BF16),
                        pltpu.VMEM((2, tile, dv), BF16),
                        pltpu.VMEM((2, tile, dv), BF16),
                        pltpu.VMEM((2, tile, SMALL_W), F32),
                        pltpu.VMEM((SSD_STATE, d), F32),
                        pltpu.VMEM((TAIL, cdim), BF16),
                        pltpu.VMEM((n_chunks + 1, hk, dv), F32),
                        pltpu.VMEM((CHUNK, hk), F32),
                        pltpu.VMEM((CHUNK, hk), F32),
                        pltpu.VMEM((CHUNK, CHUNK), BF16),
                        pltpu.VMEM((CHUNK, CHUNK), F32),
                        pltpu.VMEM((CHUNK, CHUNK), F32)],
        compiler_params=pltpu.CompilerParams(
            dimension_semantics=("arbitrary",), vmem_limit_bytes=VMEM_LIMIT),
        name="mixer",
    )(h2d, h2d, p["norm_w"], p["w_main"], p["w_small"],
      p["conv_w"], p["conv_b"], p["dt_bias"], p["a_log"], p["d_skip"], p["ssd_norm_w"], s0, t0,
      p["gate_w2"], p["gate_b"], p["gla_norm_w"], g0)


def _merge_body(ys_ref, yg_ref, mg_ref, h_ref, wb_ref, wo_ref, fnw_ref, o_ref, *, final_norm):
    d = h_ref.shape[-1]
    bp0 = _dot(ys_ref[...], wb_ref[0])
    bp1 = _dot(yg_ref[...], wb_ref[1])
    g0 = _sigmoid(mg_ref[:, :d].astype(F32))
    g1 = _sigmoid(mg_ref[:, d:].astype(F32))
    merged = (g0 * bp0 + g1 * bp1).astype(BF16)
    out = h_ref[...] + _dot(merged, wo_ref[...])
    if final_norm:
        out = _rmsnorm(out, fnw_ref[...])
    o_ref[...] = out


def _merge(ys, yg, mg, h2d, w_branch, w_out, final_norm_w, tm, final_norm):
    m, d = h2d.shape
    row = lambda i: (i, 0)
    return pl.pallas_call(
        functools.partial(_merge_body, final_norm=final_norm),
        out_shape=jax.ShapeDtypeStruct((m, d), F32),
        grid=(m // tm,),
        in_specs=[
            pl.BlockSpec((tm, d), row),
            pl.BlockSpec((tm, d), row),
            pl.BlockSpec((tm, 2 * d), row),
            pl.BlockSpec((tm, d), row),
            pl.BlockSpec(w_branch.shape, lambda i: (0, 0, 0)),
            pl.BlockSpec(w_out.shape, lambda i: (0, 0)),
            pl.BlockSpec((1, d), lambda i: (0, 0)),
        ],
        out_specs=pl.BlockSpec((tm, d), row),
        compiler_params=pltpu.CompilerParams(
            dimension_semantics=("arbitrary",), vmem_limit_bytes=VMEM_LIMIT),
        name="merge",
    )(ys, yg, mg, h2d, w_branch, w_out, final_norm_w)


def _pad_lanes(v, width):
    v = v.reshape(1, -1)
    return jnp.pad(v, ((0, 0), (0, width - v.shape[1])))


def _layer_params(i, norm_w, w_in, conv_w, conv_b, dt_bias, a_log, d_skip, ssd_norm_w,
                  gla_gate_w2, gla_gate_b, gla_norm_w, w_branch, w_out):
    d = w_out.shape[-1]
    heads = dt_bias.shape[-1]
    rank = gla_gate_w2.shape[1]
    dk = gla_gate_w2.shape[2]
    dv = d
    cdim = conv_w.shape[-1]
    widths = (d, cdim, heads, dk, dk, dv, dv, rank, 2 * d)
    starts = [0]
    for w in widths:
        starts.append(starts[-1] + w)
    col = lambda j: w_in[i][:, starts[j]:starts[j + 1]]
    w_main = jnp.concatenate([col(0), col(1), col(3), col(4), col(5), col(6), col(8)], axis=1).astype(BF16)
    assert heads + rank <= SMALL_W
    w_small = jnp.concatenate(
        [col(2), col(7), jnp.zeros((d, SMALL_W - heads - rank), F32)], axis=1).astype(BF16)
    dtb, alog = dt_bias[i], a_log[i]
    w2 = jnp.zeros((SMALL_W, dk), F32).at[heads:heads + rank].set(gla_gate_w2[i])
    return dict(
        norm_w=norm_w[i].reshape(1, d), w_main=w_main, w_small=w_small,
        conv_w=conv_w[i], conv_b=conv_b[i].reshape(1, cdim),
        dt_bias=_pad_lanes(dtb, SMALL_W), a_log=_pad_lanes(alog, SMALL_W),
        d_skip=jnp.repeat(d_skip[i], SSD_HEAD_DIM).reshape(1, d),
        ssd_norm_w=ssd_norm_w[i].reshape(1, d),
        gate_w2=w2, gate_b=gla_gate_b[i].reshape(1, dk), gla_norm_w=gla_norm_w[i].reshape(1, -1),
        w_branch=w_branch[i].astype(BF16), w_out=w_out[i].astype(BF16))


def _layer(h, p, states, tile, tm, n_valid, final_norm_w, final_norm, emit_state):
    b, l, d = h.shape
    outs = _mixer(h, p, states, tile, n_valid, emit_state)
    y_ssd, y_gla, mg = outs[:3]
    out = _merge(y_ssd, y_gla, mg, h.reshape(b * l, d), p["w_branch"], p["w_out"], final_norm_w, tm, final_norm)
    final = (outs[3][0], outs[4][0], outs[5][0]) if emit_state else None
    return out.reshape(b, l, d), final


def kernel(x, meta_tokens, norm_w, w_in, conv_w, conv_b, dt_bias, a_log, d_skip, ssd_norm_w, gla_gate_w2,
           gla_gate_b, gla_norm_w, w_branch, w_out, final_norm_w):
    b, seq, d = x.shape
    depth = norm_w.shape[0]
    n_meta = meta_tokens.shape[0]
    assert TAIL <= n_meta <= CHUNK and n_meta % TAIL == 0 and seq % CHUNK == 0
    tile = 512 if seq % 512 == 0 else 2 * CHUNK
    tm = 512 if (b * seq) % 512 == 0 else CHUNK
    cdim = conv_w.shape[-1]
    dk = gla_gate_w2.shape[2]
    fnw = final_norm_w.reshape(1, d)
    zero_states = (jnp.zeros((SSD_STATE, d), F32), jnp.zeros((TAIL, cdim), BF16),
                   jnp.zeros((dk // GLA_HEADS, d), F32))
    h_meta = jnp.pad(meta_tokens.astype(x.dtype), ((0, CHUNK - n_meta), (0, 0)))[None]
    h = x
    for i in range(depth):
        p = _layer_params(i, norm_w, w_in, conv_w, conv_b, dt_bias, a_log, d_skip, ssd_norm_w,
                          gla_gate_w2, gla_gate_b, gla_norm_w, w_branch, w_out)
        h_meta, seeded = _layer(h_meta, p, zero_states, CHUNK, CHUNK, n_meta, fnw, False, True)
        h, _ = _layer(h, p, seeded, tile, tm, tile, fnw, i == depth - 1, False)
    return h
```

```python
import functools

import jax
import jax.numpy as jnp
from jax import lax
from jax.experimental import pallas as pl
from jax.experimental.pallas import tpu as pltpu

F32 = jnp.float32
BF16 = jnp.bfloat16

SSD_HEAD_DIM = 64
SSD_GROUPS = 4
SSD_STATE = 128
GLA_HEADS = 4
GLA_GATE_TEMP = 16.0
EPS = 1e-6

CHUNK = 128
SMALL_W = 128
SUBLANES = 8
TAIL = 16
PROJ_LANES = 512
NEG_BIG = -1e30
GLA_SAFE_LOG_DECAY = -60.0
FILL_STRIDE = 2
MIXER_TILE = 4 * CHUNK
MERGE_TILE = 8 * CHUNK
VMEM_LIMIT = 60 * 1024 * 1024


def _sigmoid(x):
    return 0.5 * (jnp.tanh(0.5 * x) + 1.0)


def _silu(x):
    h = 0.5 * x
    return h + h * jnp.tanh(h)


def _softplus(x):
    return jnp.maximum(x, 0.0) + jnp.log(1.0 + jnp.exp(-jnp.abs(x)))


def _rmsnorm(x, w):
    return x * lax.rsqrt(jnp.mean(x * x, axis=-1, keepdims=True) + EPS) * w


def _split2(x):
    hi = x.astype(BF16)
    mid = (x - hi.astype(F32)).astype(BF16)
    return hi, mid


def _dot(a, b):
    return jnp.dot(a, b, preferred_element_type=F32)


def _dot_nt(a, b):
    return lax.dot_general(a, b, (((1,), (1,)), ((), ())), preferred_element_type=F32)


def _dot_tn(a, b):
    return lax.dot_general(a, b, (((0,), (0,)), ((), ())), preferred_element_type=F32)


def _cumsum_rows(tri_bf, x):
    hi, mid = _split2(x)
    return _dot(tri_bf, hi) + _dot(tri_bf, mid)


def _round_robin(generators, fillers, n_fill, stride):
    generators = list(generators)
    n_fill = min(n_fill, len(fillers))
    rnd = 0
    while generators or n_fill:
        for gen in list(generators):
            if next(gen, StopIteration) is StopIteration:
                generators.remove(gen)
        if n_fill and (rnd % stride == 0 or not generators):
            fillers.pop(0)()
            n_fill -= 1
        rnd += 1


def _ssd_decays(sm, dtb_ref, alog_ref, tri_ref, out, *, n_chunks, heads, n_valid):
    head_lane = lax.broadcasted_iota(jnp.int32, (CHUNK, SMALL_W), 1) < heads
    a_full = -jnp.exp(alog_ref[...])
    dts = []
    for c in range(n_chunks):
        dt = _softplus(sm[c * CHUNK:(c + 1) * CHUNK, :] + dtb_ref[...])
        dt = jnp.where(head_lane, dt, 0.0)
        if n_valid is not None:
            row = lax.broadcasted_iota(jnp.int32, (CHUNK, SMALL_W), 0) + c * CHUNK
            dt = jnp.where(row < n_valid, dt, 0.0)
        dts.append(dt)
    yield
    cum_wide = _cumsum_rows(tri_ref[...], jnp.concatenate([dt * a_full for dt in dts], axis=1))
    yield

    for c, dt in enumerate(dts):
        cum = cum_wide[:, c * SMALL_W:(c + 1) * SMALL_W]
        cum_last = cum[CHUNK - 1:CHUNK, :]
        out.append((cum, cum.T, dt.T, jnp.exp(cum), jnp.exp(cum_last - cum) * dt))
    yield


def _ssd_chunk(xbc, z, decay, prm, negmask_ref, y, rows, state, tail):
    cw_ref, cb_ref, dsk_ref, nw_ref = prm
    d_ssd = z.shape[-1]
    cdim = xbc.shape[-1]
    heads = d_ssd // SSD_HEAD_DIM
    gw = d_ssd // SSD_GROUPS
    hpg = heads // SSD_GROUPS
    gs = SSD_STATE
    n_taps = cw_ref.shape[0]

    def conv_silu(lanes):
        cur = xbc[:, lanes].astype(F32)
        hist = tail[:, lanes].astype(F32)[TAIL - SUBLANES:, :]
        ext = jnp.concatenate([hist, cur], axis=0)
        xc = cb_ref[:, lanes] + cw_ref[n_taps - 1:n_taps, lanes] * cur
        for j in range(1, n_taps):
            xc = xc + cw_ref[n_taps - 1 - j:n_taps - j, lanes] * pltpu.roll(ext, j, axis=0)[SUBLANES:, :]
        return _silu(xc)

    bm = conv_silu(slice(d_ssd, d_ssd + SSD_GROUPS * gs)).astype(BF16)
    yield
    cm = conv_silu(slice(d_ssd + SSD_GROUPS * gs, cdim)).astype(BF16)
    yield
    cum, cum_t, dt_t, e_split, w_split = decay

    blk_lane = lax.broadcasted_iota(jnp.int32, (CHUNK, gw), 1) // SSD_HEAD_DIM
    for g in range(SSD_GROUPS):
        lanes = slice(g * gw, (g + 1) * gw)
        bg = bm[:, g * gs:(g + 1) * gs]
        cg = cm[:, g * gs:(g + 1) * gs]
        xs = conv_silu(lanes)
        xs_bf = xs.astype(BF16)
        def spread(v):
            wide = jnp.broadcast_to(v[:, g * hpg:g * hpg + 1], (CHUNK, gw))
            for j in range(1, hpg):
                wide = jnp.where(blk_lane == j, jnp.broadcast_to(v[:, g * hpg + j:g * hpg + j + 1], (CHUNK, gw)), wide)
            return wide

        e_g = spread(e_split)
        xw = (xs * spread(w_split)).astype(BF16)
        cb_g = _dot_nt(cg, bg)
        yield
        m_parts, x_parts = [], []
        for j in range(hpg):
            h = g * hpg + j
            seg = (cum[:, h:h + 1] - cum_t[h:h + 1, :]) + negmask_ref[...]
            m_parts.append((cb_g * jnp.exp(seg) * dt_t[h:h + 1, :]).astype(BF16))
            x_parts.append(jnp.where(blk_lane == j, xs_bf, jnp.zeros_like(xs_bf)))
            if j % 2 == 1:
                yield
        y_diag = _dot(jnp.concatenate(m_parts, axis=1), jnp.concatenate(x_parts, axis=0))
        s_g = state[:, lanes]
        y_off = _dot(cg, s_g.astype(BF16))
        state[:, lanes] = s_g * e_g[CHUNK - 1:CHUNK, :] + _dot_tn(bg, xw)
        yg = y_diag + y_off * e_g + dsk_ref[:, lanes] * xs
        yg = yg * _silu(z[:, lanes].astype(F32))
        y[rows, lanes] = _rmsnorm(yg, nw_ref[:, lanes]).astype(y.dtype)
        yield

    tail[...] = xbc[CHUNK - TAIL:CHUNK, :]


def _gla_decays(sm, w2_ref, gb_ref, tri_ref, out, *, n_chunks, n_valid):
    dk = w2_ref.shape[-1]
    logit = _dot(sm[...].astype(BF16), w2_ref[...].astype(BF16)) + gb_ref[...]
    yield
    log_a = -_softplus(-logit) * (1.0 / GLA_GATE_TEMP)
    if n_valid is not None:
        row = lax.broadcasted_iota(jnp.int32, log_a.shape, 0)
        log_a = jnp.where(row < n_valid, log_a, 0.0)
    yield
    g_wide = _cumsum_rows(
        tri_ref[...], jnp.concatenate([log_a[c * CHUNK:(c + 1) * CHUNK, :] for c in range(n_chunks)], axis=1))
    out.extend(g_wide[:, c * dk:(c + 1) * dk] for c in range(n_chunks))
    yield


class _GlaChunk:
    def __init__(self, q, k, v, gg, g, nw_ref, tri01_ref, y, rows, row0, s_old, s_new, kf_scr, g_scr, *, n_valid):
        self.q, self.k, self.v, self.gg, self.g = q, k, v, gg, g
        self.nw_ref, self.tri01_ref = nw_ref, tri01_ref
        self.y, self.rows, self.row0 = y, rows, row0
        self.s_old, self.s_new = s_old, s_new
        self.kf_scr, self.g_scr = kf_scr, g_scr
        self.n_valid = n_valid
        self.hk = q.shape[-1] // GLA_HEADS
        self.hv = v.shape[-1] // GLA_HEADS
        self.g_mins = []

    def _lanes(self, h):
        return slice(h * self.hk, (h + 1) * self.hk), slice(h * self.hv, (h + 1) * self.hv)

    def _gate_decay(self, ks):
        kf = self.k[:, ks].astype(F32)
        if self.n_valid is not None:
            row = lax.broadcasted_iota(jnp.int32, (CHUNK, self.hk), 0) + self.row0
            kf = jnp.where(row < self.n_valid, kf, 0.0)
        qs = self.q[:, ks].astype(F32) * (self.hk ** -0.5)
        return qs, kf, self.g[:, ks]

    def _emit(self, h, q_dec, att):
        ks, vs = self._lanes(h)
        att = (att * self.tri01_ref[...]).astype(BF16)
        o = _dot(jnp.concatenate([q_dec, att], axis=1),
                 jnp.concatenate([self.s_old[:, vs].astype(BF16), self.v[:, vs]], axis=0))
        gate = _silu(self.gg[:, vs].astype(F32))
        self.y[self.rows, vs] = (_rmsnorm(o, self.nw_ref[...]) * gate).astype(self.y.dtype)

    def stages(self):
        for h in range(GLA_HEADS):
            ks, vs = self._lanes(h)
            qs, kf, g = self._gate_decay(ks)
            yield
            g_last = g[CHUNK - 1:CHUNK, :]
            q_dec = (qs * jnp.exp(g)).astype(BF16)
            k_inv = (kf * jnp.exp(jnp.minimum(-g, -GLA_SAFE_LOG_DECAY))).astype(BF16)
            k_dec = (kf * jnp.exp(g_last - g)).astype(BF16)
            att = _dot_nt(q_dec, k_inv)
            self.g_mins.append(jnp.min(g_last))
            yield
            self._emit(h, q_dec, att)
            decay_col = jnp.broadcast_to(jnp.exp(g_last), (CHUNK, self.hk)).T
            decay_col = jnp.concatenate([decay_col] * (self.hv // self.hk), axis=1)
            self.s_new[:, vs] = self.s_old[:, vs] * decay_col + _dot_tn(k_dec, self.v[:, vs])
            yield

    def redo_unsafe(self):
        g_min = functools.reduce(jnp.minimum, self.g_mins)

        @pl.when(g_min < GLA_SAFE_LOG_DECAY)
        def _():
            col_id = lax.broadcasted_iota(jnp.int32, (CHUNK, CHUNK), 1)
            for h in range(GLA_HEADS):
                ks, _ = self._lanes(h)
                q_h, kf, g_h = self._gate_decay(ks)
                self.kf_scr[...] = kf
                self.g_scr[...] = g_h

                def key_rows(sb, att, q_h=q_h, g_h=g_h):
                    s0 = pl.multiple_of(sb * SUBLANES, SUBLANES)
                    k_rows = self.kf_scr[pl.ds(s0, SUBLANES), :]
                    g_rows = self.g_scr[pl.ds(s0, SUBLANES), :]
                    for j in range(SUBLANES):
                        p = q_h * k_rows[j:j + 1, :] * jnp.exp(jnp.minimum(g_h - g_rows[j:j + 1, :], 0.0))
                        att = jnp.where(col_id == s0 + j, jnp.sum(p, axis=1, keepdims=True), att)
                    return att

                att = lax.fori_loop(0, CHUNK // SUBLANES, key_rows, jnp.zeros((CHUNK, CHUNK), F32))
                self._emit(h, (q_h * jnp.exp(g_h)).astype(BF16), att)


def _mixer_body(h_ref, hn_ref, nw_ref, wm_ref,
                cw_ref, cb_ref, dtb_ref, alog_ref, dsk_ref, snw_ref, s0_ref, t0_ref,
                w2_ref, gb_ref, gnw_ref, g0_ref,
                ys_ref, yg_ref, mg_ref, sf_ref, tf_ref, gf_ref,
                u_s, z_s, xbc_s, q_s, k_s, v_s, gg_s, sm_s, ssd_state, tail, gla_state, kf_scr, g_scr,
                tri_s, tri01_s, negmask_s,
                *, tile, n_valid, tiles_per_seq):
    i = pl.program_id(0)
    n_chunks = tile // CHUNK
    d = h_ref.shape[-1]
    heads = d // SSD_HEAD_DIM
    cur = i % 2
    nxt = 1 - cur

    slots = (("z", z_s), ("xbc", xbc_s), ("q", q_s), ("k", k_s), ("v", v_s), ("gg", gg_s))
    off = {}
    start = SMALL_W
    for name, ref in slots:
        off[name] = start
        start += ref.shape[-1]
    off_mg = start

    def normalise_jobs(src_ref, slot):
        def job(r0):
            def run():
                u_s[slot, r0:r0 + CHUNK, :] = _rmsnorm(src_ref[r0:r0 + CHUNK, :], nw_ref[...]).astype(BF16)
            return run
        return [job(r0) for r0 in range(0, tile, CHUNK)]

    def project_jobs(slot):
        def small():
            sm_s[slot] = _dot(u_s[slot], wm_ref[:, :SMALL_W])
        jobs = [small]
        for name in ("xbc", "z", "q", "k", "v", "gg"):
            ref = dict(slots)[name]
            width = ref.shape[-1]
            step = min(width, PROJ_LANES)
            for c0 in range(0, width, step):
                def job(ref=ref, c0=c0, step=step, o=off[name]):
                    ref[slot, :, c0:c0 + step] = _dot(u_s[slot], wm_ref[:, o + c0:o + c0 + step]).astype(ref.dtype)
                jobs.append(job)
        return jobs

    def merge_gate_jobs(slot):
        jobs = []
        for c0 in range(0, mg_ref.shape[-1], PROJ_LANES):
            def job(c0=c0):
                mg_ref[:, c0:c0 + PROJ_LANES] = _dot(
                    u_s[slot], wm_ref[:, off_mg + c0:off_mg + c0 + PROJ_LANES]).astype(mg_ref.dtype)
            jobs.append(job)
        return jobs

    @pl.when(i == 0)
    def _():
        tr = lax.broadcasted_iota(jnp.int32, (CHUNK, CHUNK), 0)
        tc = lax.broadcasted_iota(jnp.int32, (CHUNK, CHUNK), 1)
        tri_s[...] = (tr >= tc).astype(BF16)
        tri01_s[...] = (tr >= tc).astype(F32)
        negmask_s[...] = jnp.where(tr >= tc, 0.0, NEG_BIG)
        for job in normalise_jobs(h_ref, 0) + project_jobs(0):
            job()

    @pl.when(i % tiles_per_seq == 0)
    def _():
        ssd_state[...] = s0_ref[...]
        tail[...] = t0_ref[...]
        gla_state[0] = g0_ref[...]

    ssd_prm = (cw_ref, cb_ref, dsk_ref, snw_ref)
    nv = None if n_valid == tile else n_valid

    fillers = normalise_jobs(hn_ref, nxt) + project_jobs(nxt) + merge_gate_jobs(cur)
    ssd_decays, gla_decays = [], []
    _round_robin(
        [_ssd_decays(sm_s.at[cur], dtb_ref, alog_ref, tri_s, ssd_decays, n_chunks=n_chunks, heads=heads, n_valid=nv),
         _gla_decays(sm_s.at[cur], w2_ref, gb_ref, tri_s, gla_decays, n_chunks=n_chunks, n_valid=nv)],
        fillers, 0, 1)
    per_chunk = -(-len(fillers) // n_chunks)
    glas = []
    for c in range(n_chunks):
        r0 = c * CHUNK
        rows = slice(r0, r0 + CHUNK)
        view = lambda ref: ref.at[cur, rows]
        gla = _GlaChunk(view(q_s), view(k_s), view(v_s), view(gg_s), gla_decays[c], gnw_ref, tri01_s,
                        yg_ref, rows, r0, gla_state.at[c], gla_state.at[c + 1], kf_scr, g_scr, n_valid=nv)
        ssd = _ssd_chunk(view(xbc_s), view(z_s), ssd_decays[c], ssd_prm, negmask_s, ys_ref, rows, ssd_state, tail)
        _round_robin([ssd, gla.stages()], fillers, per_chunk, FILL_STRIDE)
        glas.append(gla)
    assert not fillers

    for gla in glas:
        gla.redo_unsafe()
    gla_state[0] = gla_state[n_chunks]

    if sf_ref is not None:
        @pl.when(i % tiles_per_seq == tiles_per_seq - 1)
        def _():
            sf_ref[0] = ssd_state[...]
            tf_ref[0] = xbc_s[cur, n_valid - TAIL:n_valid, :]
            gf_ref[0] = gla_state[n_chunks]


def _mixer_body_no_state(*refs, **kw):
    n_in, n_out = 16, 3
    ins, outs, scratch = refs[:n_in], refs[n_in:n_in + n_out], refs[n_in + n_out:]
    _mixer_body(*ins, *outs, None, None, None, *scratch, **kw)


def _mixer(h, p, states, tile, n_valid, emit_state):
    b, l, d = h.shape
    cdim = p["conv_w"].shape[-1]
    assert p["conv_w"].shape[0] - 1 <= SUBLANES
    dk = p["gate_b"].shape[-1]
    dv = d
    hk = dk // GLA_HEADS
    n_main = p["w_main"].shape[1]
    tiles_per_seq = l // tile
    n_tiles = b * tiles_per_seq
    n_chunks = tile // CHUNK
    s0, t0, g0 = states
    h2d = h.reshape(b * l, d)
    tok = lambda i: (i, 0)
    nxt = lambda i: (jnp.minimum(i + 1, n_tiles - 1), 0)
    const = lambda i: (0, 0)
    per_b = lambda i: (i // tiles_per_seq, 0, 0)
    whole = lambda shape: pl.BlockSpec(shape, const)
    resident = lambda shape: pl.BlockSpec(shape, const, pipeline_mode=pl.Buffered(1))
    out_shape = [jax.ShapeDtypeStruct((b * l, d), BF16),
                 jax.ShapeDtypeStruct((b * l, dv), BF16),
                 jax.ShapeDtypeStruct((b * l, 2 * d), BF16)]
    out_specs = [pl.BlockSpec((tile, d), tok), pl.BlockSpec((tile, dv), tok), pl.BlockSpec((tile, 2 * d), tok)]
    if emit_state:
        out_shape += [jax.ShapeDtypeStruct((b, SSD_STATE, d), F32),
                      jax.ShapeDtypeStruct((b, TAIL, cdim), BF16),
                      jax.ShapeDtypeStruct((b, hk, dv), F32)]
        out_specs += [pl.BlockSpec((1, SSD_STATE, d), per_b),
                      pl.BlockSpec((1, TAIL, cdim), per_b),
                      pl.BlockSpec((1, hk, dv), per_b)]
    kw = dict(tile=tile, n_valid=n_valid, tiles_per_seq=tiles_per_seq)
    body = functools.partial(_mixer_body if emit_state else _mixer_body_no_state, **kw)
    return pl.pallas_call(
        body,
        out_shape=out_shape,
        grid=(n_tiles,),
        in_specs=[
            resident((tile, d)),
            pl.BlockSpec((tile, d), nxt),
            whole((1, d)),
            resident((d, n_main)),
            whole(p["conv_w"].shape), whole((1, cdim)), whole((1, SMALL_W)), whole((1, SMALL_W)),
            whole((1, d)), whole((1, d)), whole((SSD_STATE, d)), whole((TAIL, cdim)),
            whole((SMALL_W, dk)), whole((1, dk)), whole((1, dv // GLA_HEADS)), whole((hk, dv)),
        ],
        out_specs=out_specs,
        scratch_shapes=[pltpu.VMEM((2, tile, d), BF16),
                        pltpu.VMEM((2, tile, d), BF16),
                        pltpu.VMEM((2, tile, cdim), BF16),
                        pltpu.VMEM((2, tile, dk), BF16),
                        pltpu.VMEM((2, tile, dk), BF16),
                        pltpu.VMEM((2, tile, dv), BF16),
                        pltpu.VMEM((2, tile, dv), BF16),
                        pltpu.VMEM((2, tile, SMALL_W), F32),
                        pltpu.VMEM((SSD_STATE, d), F32),
                        pltpu.VMEM((TAIL, cdim), BF16),
                        pltpu.VMEM((n_chunks + 1, hk, dv), F32),
                        pltpu.VMEM((CHUNK, hk), F32),
                        pltpu.VMEM((CHUNK, hk), F32),
                        pltpu.VMEM((CHUNK, CHUNK), BF16),
                        pltpu.VMEM((CHUNK, CHUNK), F32),
                        pltpu.VMEM((CHUNK, CHUNK), F32)],
        compiler_params=pltpu.CompilerParams(
            dimension_semantics=("arbitrary",), vmem_limit_bytes=VMEM_LIMIT),
        name="mixer",
    )(h2d, h2d, p["norm_w"], p["w_main"],
      p["conv_w"], p["conv_b"], p["dt_bias"], p["a_log"], p["d_skip"], p["ssd_norm_w"], s0, t0,
      p["gate_w2"], p["gate_b"], p["gla_norm_w"], g0)


def _merge_body(ys_ref, yg_ref, mg_ref, h_ref, wb_ref, wo_ref, fnw_ref, o_ref, *, final_norm):
    d = h_ref.shape[-1]
    bp0 = _dot(ys_ref[...], wb_ref[0])
    bp1 = _dot(yg_ref[...], wb_ref[1])
    g0 = _sigmoid(mg_ref[:, :d].astype(F32))
    g1 = _sigmoid(mg_ref[:, d:].astype(F32))
    merged = (g0 * bp0 + g1 * bp1).astype(BF16)
    out = h_ref[...] + _dot(merged, wo_ref[...])
    if final_norm:
        out = _rmsnorm(out, fnw_ref[...])
    o_ref[...] = out


def _merge(ys, yg, mg, h2d, w_branch, w_out, final_norm_w, tm, final_norm):
    m, d = h2d.shape
    row = lambda i: (i, 0)
    return pl.pallas_call(
        functools.partial(_merge_body, final_norm=final_norm),
        out_shape=jax.ShapeDtypeStruct((m, d), F32),
        grid=(m // tm,),
        in_specs=[
            pl.BlockSpec((tm, d), row),
            pl.BlockSpec((tm, d), row),
            pl.BlockSpec((tm, 2 * d), row),
            pl.BlockSpec((tm, d), row),
            pl.BlockSpec(w_branch.shape, lambda i: (0, 0, 0)),
            pl.BlockSpec(w_out.shape, lambda i: (0, 0)),
            pl.BlockSpec((1, d), lambda i: (0, 0)),
        ],
        out_specs=pl.BlockSpec((tm, d), row),
        compiler_params=pltpu.CompilerParams(
            dimension_semantics=("arbitrary",), vmem_limit_bytes=VMEM_LIMIT),
        name="merge",
    )(ys, yg, mg, h2d, w_branch, w_out, final_norm_w)


def _pad_lanes(v, width):
    v = v.reshape(1, -1)
    return jnp.pad(v, ((0, 0), (0, width - v.shape[1])))


def _layer_params(i, norm_w, w_in, conv_w, conv_b, dt_bias, a_log, d_skip, ssd_norm_w,
                  gla_gate_w2, gla_gate_b, gla_norm_w, w_branch, w_out):
    d = w_out.shape[-1]
    heads = dt_bias.shape[-1]
    rank = gla_gate_w2.shape[1]
    dk = gla_gate_w2.shape[2]
    dv = d
    cdim = conv_w.shape[-1]
    widths = (d, cdim, heads, dk, dk, dv, dv, rank, 2 * d)
    starts = [0]
    for w in widths:
        starts.append(starts[-1] + w)
    col = lambda j: w_in[i][:, starts[j]:starts[j + 1]]
    assert heads + rank <= SMALL_W
    w_main = jnp.concatenate(
        [col(2), col(7), jnp.zeros((d, SMALL_W - heads - rank), F32),
         col(0), col(1), col(3), col(4), col(5), col(6), col(8)], axis=1).astype(BF16)
    dtb, alog = dt_bias[i], a_log[i]
    w2 = jnp.pad(gla_gate_w2[i], ((heads, SMALL_W - heads - rank), (0, 0)))
    return dict(
        norm_w=norm_w[i].reshape(1, d), w_main=w_main,
        conv_w=conv_w[i], conv_b=conv_b[i].reshape(1, cdim),
        dt_bias=_pad_lanes(dtb, SMALL_W), a_log=_pad_lanes(alog, SMALL_W),
        d_skip=jnp.repeat(d_skip[i], SSD_HEAD_DIM).reshape(1, d),
        ssd_norm_w=ssd_norm_w[i].reshape(1, d),
        gate_w2=w2, gate_b=gla_gate_b[i].reshape(1, dk), gla_norm_w=gla_norm_w[i].reshape(1, -1),
        w_branch=w_branch[i].astype(BF16), w_out=w_out[i].astype(BF16))


def _layer(h, p, states, tile, tm, n_valid, final_norm_w, final_norm, emit_state):
    b, l, d = h.shape
    outs = _mixer(h, p, states, tile, n_valid, emit_state)
    y_ssd, y_gla, mg = outs[:3]
    out = _merge(y_ssd, y_gla, mg, h.reshape(b * l, d), p["w_branch"], p["w_out"], final_norm_w, tm, final_norm)
    final = (outs[3][0], outs[4][0], outs[5][0]) if emit_state else None
    return out.reshape(b, l, d), final


def kernel(x, meta_tokens, norm_w, w_in, conv_w, conv_b, dt_bias, a_log, d_skip, ssd_norm_w, gla_gate_w2,
           gla_gate_b, gla_norm_w, w_branch, w_out, final_norm_w):
    b, seq, d = x.shape
    depth = norm_w.shape[0]
    n_meta = meta_tokens.shape[0]
    assert TAIL <= n_meta <= CHUNK and n_meta % TAIL == 0 and seq % CHUNK == 0
    tile = MIXER_TILE if seq % MIXER_TILE == 0 else CHUNK
    tm = MERGE_TILE if (b * seq) % MERGE_TILE == 0 else CHUNK
    cdim = conv_w.shape[-1]
    dk = gla_gate_w2.shape[2]
    fnw = final_norm_w.reshape(1, d)
    zero_states = (jnp.zeros((SSD_STATE, d), F32), jnp.zeros((TAIL, cdim), BF16),
                   jnp.zeros((dk // GLA_HEADS, d), F32))
    h_meta = jnp.pad(meta_tokens.astype(x.dtype), ((0, CHUNK - n_meta), (0, 0)))[None]
    h = x
    for i in range(depth):
        p = _layer_params(i, norm_w, w_in, conv_w, conv_b, dt_bias, a_log, d_skip, ssd_norm_w,
                          gla_gate_w2, gla_gate_b, gla_norm_w, w_branch, w_out)
        h_meta, seeded = _layer(h_meta, p, zero_states, CHUNK, CHUNK, n_meta, fnw, False, True)
        h, _ = _layer(h, p, seeded, tile, tm, tile, fnw, i == depth - 1, False)
    return h
```

```python
import functools

import jax
import jax.numpy as jnp
from jax import lax
from jax.experimental import pallas as pl
from jax.experimental.pallas import tpu as pltpu

F32 = jnp.float32
BF16 = jnp.bfloat16

SSD_HEAD_DIM = 64
SSD_GROUPS = 4
SSD_STATE = 128
GLA_HEADS = 4
GLA_GATE_TEMP = 16.0
EPS = 1e-6

CHUNK = 128
SMALL_W = 128
SUBLANES = 8
TAIL = 16
PROJ_LANES = 512
NEG_BIG = -1e30
GLA_SAFE_LOG_DECAY = -60.0
FILL_STRIDE = 2
MIXER_TILE = 4 * CHUNK
MERGE_TILE = 8 * CHUNK
VMEM_LIMIT = 62 * 1024 * 1024


def _sigmoid(x):
    return 0.5 * (jnp.tanh(0.5 * x) + 1.0)


def _silu(x):
    h = 0.5 * x
    return h + h * jnp.tanh(h)


def _softplus(x):
    return jnp.maximum(x, 0.0) + jnp.log(1.0 + jnp.exp(-jnp.abs(x)))


def _rmsnorm(x, w):
    return x * lax.rsqrt(jnp.mean(x * x, axis=-1, keepdims=True) + EPS) * w


def _split2(x):
    hi = x.astype(BF16)
    mid = (x - hi.astype(F32)).astype(BF16)
    return hi, mid


def _dot(a, b):
    return jnp.dot(a, b, preferred_element_type=F32)


def _dot_nt(a, b):
    return lax.dot_general(a, b, (((1,), (1,)), ((), ())), preferred_element_type=F32)


def _dot_tn(a, b):
    return lax.dot_general(a, b, (((0,), (0,)), ((), ())), preferred_element_type=F32)


def _cumsum_rows(tri_bf, x):
    hi, mid = _split2(x)
    return _dot(tri_bf, hi) + _dot(tri_bf, mid)


def _round_robin(generators, fillers, n_fill, stride):
    generators = list(generators)
    n_fill = min(n_fill, len(fillers))
    rnd = 0
    while generators or n_fill:
        for gen in list(generators):
            if next(gen, StopIteration) is StopIteration:
                generators.remove(gen)
        if n_fill and (rnd % stride == 0 or not generators):
            fillers.pop(0)()
            n_fill -= 1
        rnd += 1


def _ssd_decays(sm, dtb_ref, alog_ref, tri_ref, out, *, n_chunks, heads, n_valid):
    head_lane = lax.broadcasted_iota(jnp.int32, (CHUNK, SMALL_W), 1) < heads
    a_full = -jnp.exp(alog_ref[...])
    dts = []
    for c in range(n_chunks):
        dt = _softplus(sm[c * CHUNK:(c + 1) * CHUNK, :] + dtb_ref[...])
        dt = jnp.where(head_lane, dt, 0.0)
        if n_valid is not None:
            row = lax.broadcasted_iota(jnp.int32, (CHUNK, SMALL_W), 0) + c * CHUNK
            dt = jnp.where(row < n_valid, dt, 0.0)
        dts.append(dt)
    yield
    cum_wide = _cumsum_rows(tri_ref[...], jnp.concatenate([dt * a_full for dt in dts], axis=1))
    yield

    for c, dt in enumerate(dts):
        cum = cum_wide[:, c * SMALL_W:(c + 1) * SMALL_W]
        cum_last = cum[CHUNK - 1:CHUNK, :]
        out.append((cum, cum.T, dt.T, jnp.exp(cum), jnp.exp(cum_last - cum) * dt))
    yield


def _ssd_chunk(xbc, z, decay, prm, negmask_ref, y, rows, state, tail):
    cw_ref, cb_ref, dsk_ref, nw_ref = prm
    d_ssd = z.shape[-1]
    cdim = xbc.shape[-1]
    heads = d_ssd // SSD_HEAD_DIM
    gw = d_ssd // SSD_GROUPS
    hpg = heads // SSD_GROUPS
    gs = SSD_STATE
    n_taps = cw_ref.shape[0]

    def conv_silu(lanes):
        cur = xbc[:, lanes].astype(F32)
        hist = tail[:, lanes].astype(F32)[TAIL - SUBLANES:, :]
        ext = jnp.concatenate([hist, cur], axis=0)
        xc = cb_ref[:, lanes] + cw_ref[n_taps - 1:n_taps, lanes] * cur
        for j in range(1, n_taps):
            xc = xc + cw_ref[n_taps - 1 - j:n_taps - j, lanes] * pltpu.roll(ext, j, axis=0)[SUBLANES:, :]
        return _silu(xc)

    bm = conv_silu(slice(d_ssd, d_ssd + SSD_GROUPS * gs)).astype(BF16)
    yield
    cm = conv_silu(slice(d_ssd + SSD_GROUPS * gs, cdim)).astype(BF16)
    yield
    cum, cum_t, dt_t, e_split, w_split = decay

    blk_lane = lax.broadcasted_iota(jnp.int32, (CHUNK, gw), 1) // SSD_HEAD_DIM
    for g in range(SSD_GROUPS):
        lanes = slice(g * gw, (g + 1) * gw)
        bg = bm[:, g * gs:(g + 1) * gs]
        cg = cm[:, g * gs:(g + 1) * gs]
        xs = conv_silu(lanes)
        xs_bf = xs.astype(BF16)
        def spread(v):
            wide = jnp.broadcast_to(v[:, g * hpg:g * hpg + 1], (CHUNK, gw))
            for j in range(1, hpg):
                wide = jnp.where(blk_lane == j, jnp.broadcast_to(v[:, g * hpg + j:g * hpg + j + 1], (CHUNK, gw)), wide)
            return wide

        e_g = spread(e_split)
        xw = (xs * spread(w_split)).astype(BF16)
        cb_g = _dot_nt(cg, bg)
        yield
        m_parts, x_parts = [], []
        for j in range(hpg):
            h = g * hpg + j
            seg = (cum[:, h:h + 1] - cum_t[h:h + 1, :]) + negmask_ref[...]
            m_parts.append((cb_g * jnp.exp(seg) * dt_t[h:h + 1, :]).astype(BF16))
            x_parts.append(jnp.where(blk_lane == j, xs_bf, jnp.zeros_like(xs_bf)))
            if j % 2 == 1:
                yield
        y_diag = _dot(jnp.concatenate(m_parts, axis=1), jnp.concatenate(x_parts, axis=0))
        s_g = state[:, lanes]
        y_off = _dot(cg, s_g.astype(BF16))
        state[:, lanes] = s_g * e_g[CHUNK - 1:CHUNK, :] + _dot_tn(bg, xw)
        yg = y_diag + y_off * e_g + dsk_ref[:, lanes] * xs
        yg = yg * _silu(z[:, lanes].astype(F32))
        y[rows, lanes] = _rmsnorm(yg, nw_ref[:, lanes]).astype(y.dtype)
        yield

    tail[...] = xbc[CHUNK - TAIL:CHUNK, :]


def _gla_decays(sm, w2_ref, gb_ref, tri_ref, out, *, n_chunks, n_valid):
    dk = w2_ref.shape[-1]
    logit = _dot(sm[...].astype(BF16), w2_ref[...].astype(BF16)) + gb_ref[...]
    yield
    log_a = -_softplus(-logit) * (1.0 / GLA_GATE_TEMP)
    if n_valid is not None:
        row = lax.broadcasted_iota(jnp.int32, log_a.shape, 0)
        log_a = jnp.where(row < n_valid, log_a, 0.0)
    yield
    g_wide = _cumsum_rows(
        tri_ref[...], jnp.concatenate([log_a[c * CHUNK:(c + 1) * CHUNK, :] for c in range(n_chunks)], axis=1))
    out.extend(g_wide[:, c * dk:(c + 1) * dk] for c in range(n_chunks))
    yield


class _GlaChunk:
    def __init__(self, q, k, v, gg, g, nw_ref, tri01_ref, y, rows, row0, s_old, s_new, kf_scr, g_scr, *, n_valid):
        self.q, self.k, self.v, self.gg, self.g = q, k, v, gg, g
        self.nw_ref, self.tri01_ref = nw_ref, tri01_ref
        self.y, self.rows, self.row0 = y, rows, row0
        self.s_old, self.s_new = s_old, s_new
        self.kf_scr, self.g_scr = kf_scr, g_scr
        self.n_valid = n_valid
        self.hk = q.shape[-1] // GLA_HEADS
        self.hv = v.shape[-1] // GLA_HEADS
        self.g_mins = []

    def _lanes(self, h):
        return slice(h * self.hk, (h + 1) * self.hk), slice(h * self.hv, (h + 1) * self.hv)

    def _gate_decay(self, ks):
        kf = self.k[:, ks].astype(F32)
        if self.n_valid is not None:
            row = lax.broadcasted_iota(jnp.int32, (CHUNK, self.hk), 0) + self.row0
            kf = jnp.where(row < self.n_valid, kf, 0.0)
        qs = self.q[:, ks].astype(F32) * (self.hk ** -0.5)
        return qs, kf, self.g[:, ks]

    def _emit(self, h, q_dec, att):
        ks, vs = self._lanes(h)
        att = (att * self.tri01_ref[...]).astype(BF16)
        o = _dot(jnp.concatenate([q_dec, att], axis=1),
                 jnp.concatenate([self.s_old[:, vs].astype(BF16), self.v[:, vs]], axis=0))
        gate = _silu(self.gg[:, vs].astype(F32))
        self.y[self.rows, vs] = (_rmsnorm(o, self.nw_ref[...]) * gate).astype(self.y.dtype)

    def stages(self):
        for h in range(GLA_HEADS):
            ks, vs = self._lanes(h)
            qs, kf, g = self._gate_decay(ks)
            yield
            g_last = g[CHUNK - 1:CHUNK, :]
            q_dec = (qs * jnp.exp(g)).astype(BF16)
            k_inv = (kf * jnp.exp(jnp.minimum(-g, -GLA_SAFE_LOG_DECAY))).astype(BF16)
            k_dec = (kf * jnp.exp(g_last - g)).astype(BF16)
            att = _dot_nt(q_dec, k_inv)
            self.g_mins.append(jnp.min(g_last))
            yield
            self._emit(h, q_dec, att)
            decay_col = jnp.broadcast_to(jnp.exp(g_last), (CHUNK, self.hk)).T
            decay_col = jnp.concatenate([decay_col] * (self.hv // self.hk), axis=1)
            self.s_new[:, vs] = self.s_old[:, vs] * decay_col + _dot_tn(k_dec, self.v[:, vs])
            yield

    def redo_unsafe(self):
        g_min = functools.reduce(jnp.minimum, self.g_mins)

        @pl.when(g_min < GLA_SAFE_LOG_DECAY)
        def _():
            col_id = lax.broadcasted_iota(jnp.int32, (CHUNK, CHUNK), 1)
            for h in range(GLA_HEADS):
                ks, _ = self._lanes(h)
                q_h, kf, g_h = self._gate_decay(ks)
                self.kf_scr[...] = kf
                self.g_scr[...] = g_h

                def key_rows(sb, att, q_h=q_h, g_h=g_h):
                    s0 = pl.multiple_of(sb * SUBLANES, SUBLANES)
                    k_rows = self.kf_scr[pl.ds(s0, SUBLANES), :]
                    g_rows = self.g_scr[pl.ds(s0, SUBLANES), :]
                    for j in range(SUBLANES):
                        p = q_h * k_rows[j:j + 1, :] * jnp.exp(jnp.minimum(g_h - g_rows[j:j + 1, :], 0.0))
                        att = jnp.where(col_id == s0 + j, jnp.sum(p, axis=1, keepdims=True), att)
                    return att

                att = lax.fori_loop(0, CHUNK // SUBLANES, key_rows, jnp.zeros((CHUNK, CHUNK), F32))
                self._emit(h, (q_h * jnp.exp(g_h)).astype(BF16), att)


def _mixer_body(h_ref, hn_ref, nw_ref, wm_ref,
                cw_ref, cb_ref, dtb_ref, alog_ref, dsk_ref, snw_ref, s0_ref, t0_ref,
                w2_ref, gb_ref, gnw_ref, g0_ref,
                ys_ref, yg_ref, mg_ref, sf_ref, tf_ref, gf_ref,
                u_s, z_s, xbc_s, q_s, k_s, v_s, gg_s, sm_s, ssd_state, tail, gla_state, kf_scr, g_scr,
                tri_s, tri01_s, negmask_s,
                *, tile, n_valid, tiles_per_seq):
    i = pl.program_id(0)
    n_chunks = tile // CHUNK
    d = h_ref.shape[-1]
    heads = d // SSD_HEAD_DIM
    cur = i % 2
    nxt = 1 - cur

    slots = (("z", z_s), ("xbc", xbc_s), ("q", q_s), ("k", k_s), ("v", v_s), ("gg", gg_s))
    off = {}
    start = 2 * SMALL_W
    for name, ref in slots:
        off[name] = start
        start += ref.shape[-1]
    off_mg = start

    def normalise_jobs(src_ref, slot):
        def job(r0):
            def run():
                u_s[slot, r0:r0 + CHUNK, :] = _rmsnorm(src_ref[r0:r0 + CHUNK, :], nw_ref[...]).astype(BF16)
            return run
        return [job(r0) for r0 in range(0, tile, CHUNK)]

    def project_jobs(slot):
        def small():
            sm_s[slot] = _dot(u_s[slot], wm_ref[:, :2 * SMALL_W])
        jobs = [small]
        for name in ("xbc", "z", "q", "k", "v", "gg"):
            ref = dict(slots)[name]
            width = ref.shape[-1]
            step = min(width, PROJ_LANES)
            for c0 in range(0, width, step):
                def job(ref=ref, c0=c0, step=step, o=off[name]):
                    ref[slot, :, c0:c0 + step] = _dot(u_s[slot], wm_ref[:, o + c0:o + c0 + step]).astype(ref.dtype)
                jobs.append(job)
        return jobs

    def merge_gate_jobs(slot):
        jobs = []
        for c0 in range(0, mg_ref.shape[-1], PROJ_LANES):
            def job(c0=c0):
                mg_ref[:, c0:c0 + PROJ_LANES] = _dot(
                    u_s[slot], wm_ref[:, off_mg + c0:off_mg + c0 + PROJ_LANES]).astype(mg_ref.dtype)
            jobs.append(job)
        return jobs

    @pl.when(i == 0)
    def _():
        tr = lax.broadcasted_iota(jnp.int32, (CHUNK, CHUNK), 0)
        tc = lax.broadcasted_iota(jnp.int32, (CHUNK, CHUNK), 1)
        tri_s[...] = (tr >= tc).astype(BF16)
        tri01_s[...] = (tr >= tc).astype(F32)
        negmask_s[...] = jnp.where(tr >= tc, 0.0, NEG_BIG)
        for job in normalise_jobs(h_ref, 0) + project_jobs(0):
            job()

    @pl.when(i % tiles_per_seq == 0)
    def _():
        ssd_state[...] = s0_ref[...]
        tail[...] = t0_ref[...]
        gla_state[0] = g0_ref[...]

    ssd_prm = (cw_ref, cb_ref, dsk_ref, snw_ref)
    nv = None if n_valid == tile else n_valid

    fillers = normalise_jobs(hn_ref, nxt) + project_jobs(nxt) + merge_gate_jobs(cur)
    ssd_decays, gla_decays = [], []
    _round_robin(
        [_ssd_decays(sm_s.at[cur, :, :SMALL_W], dtb_ref, alog_ref, tri_s, ssd_decays,
                     n_chunks=n_chunks, heads=heads, n_valid=nv),
         _gla_decays(sm_s.at[cur, :, SMALL_W:], w2_ref, gb_ref, tri_s, gla_decays, n_chunks=n_chunks, n_valid=nv)],
        fillers, 0, 1)
    per_chunk = -(-len(fillers) // n_chunks)
    glas = []
    for c in range(n_chunks):
        r0 = c * CHUNK
        rows = slice(r0, r0 + CHUNK)
        view = lambda ref: ref.at[cur, rows]
        gla = _GlaChunk(view(q_s), view(k_s), view(v_s), view(gg_s), gla_decays[c], gnw_ref, tri01_s,
                        yg_ref, rows, r0, gla_state.at[c], gla_state.at[c + 1], kf_scr, g_scr, n_valid=nv)
        ssd = _ssd_chunk(view(xbc_s), view(z_s), ssd_decays[c], ssd_prm, negmask_s, ys_ref, rows, ssd_state, tail)
        _round_robin([ssd, gla.stages()], fillers, per_chunk, FILL_STRIDE)
        glas.append(gla)
    assert not fillers

    for gla in glas:
        gla.redo_unsafe()
    gla_state[0] = gla_state[n_chunks]

    if sf_ref is not None:
        @pl.when(i % tiles_per_seq == tiles_per_seq - 1)
        def _():
            sf_ref[0] = ssd_state[...]
            tf_ref[0] = xbc_s[cur, n_valid - TAIL:n_valid, :]
            gf_ref[0] = gla_state[n_chunks]


def _mixer_body_no_state(*refs, **kw):
    n_in, n_out = 16, 3
    ins, outs, scratch = refs[:n_in], refs[n_in:n_in + n_out], refs[n_in + n_out:]
    _mixer_body(*ins, *outs, None, None, None, *scratch, **kw)


def _mixer(h, p, states, tile, n_valid, emit_state):
    b, l, d = h.shape
    cdim = p["conv_w"].shape[-1]
    assert p["conv_w"].shape[0] - 1 <= SUBLANES
    dk = p["gate_b"].shape[-1]
    dv = d
    hk = dk // GLA_HEADS
    n_main = p["w_main"].shape[1]
    tiles_per_seq = l // tile
    n_tiles = b * tiles_per_seq
    n_chunks = tile // CHUNK
    s0, t0, g0 = states
    h2d = h.reshape(b * l, d)
    tok = lambda i: (i, 0)
    nxt = lambda i: (jnp.minimum(i + 1, n_tiles - 1), 0)
    const = lambda i: (0, 0)
    per_b = lambda i: (i // tiles_per_seq, 0, 0)
    whole = lambda shape: pl.BlockSpec(shape, const)
    resident = lambda shape: pl.BlockSpec(shape, const, pipeline_mode=pl.Buffered(1))
    out_shape = [jax.ShapeDtypeStruct((b * l, d), BF16),
                 jax.ShapeDtypeStruct((b * l, dv), BF16),
                 jax.ShapeDtypeStruct((b * l, 2 * d), BF16)]
    out_specs = [pl.BlockSpec((tile, d), tok), pl.BlockSpec((tile, dv), tok), pl.BlockSpec((tile, 2 * d), tok)]
    if emit_state:
        out_shape += [jax.ShapeDtypeStruct((b, SSD_STATE, d), F32),
                      jax.ShapeDtypeStruct((b, TAIL, cdim), BF16),
                      jax.ShapeDtypeStruct((b, hk, dv), F32)]
        out_specs += [pl.BlockSpec((1, SSD_STATE, d), per_b),
                      pl.BlockSpec((1, TAIL, cdim), per_b),
                      pl.BlockSpec((1, hk, dv), per_b)]
    kw = dict(tile=tile, n_valid=n_valid, tiles_per_seq=tiles_per_seq)
    body = functools.partial(_mixer_body if emit_state else _mixer_body_no_state, **kw)
    return pl.pallas_call(
        body,
        out_shape=out_shape,
        grid=(n_tiles,),
        in_specs=[
            resident((tile, d)),
            pl.BlockSpec((tile, d), nxt),
            whole((1, d)),
            resident((d, n_main)),
            whole(p["conv_w"].shape), whole((1, cdim)), whole((1, SMALL_W)), whole((1, SMALL_W)),
            whole((1, d)), whole((1, d)), whole((SSD_STATE, d)), whole((TAIL, cdim)),
            whole((SMALL_W, dk)), whole((1, dk)), whole((1, dv // GLA_HEADS)), whole((hk, dv)),
        ],
        out_specs=out_specs,
        scratch_shapes=[pltpu.VMEM((2, tile, d), BF16),
                        pltpu.VMEM((2, tile, d), BF16),
                        pltpu.VMEM((2, tile, cdim), BF16),
                        pltpu.VMEM((2, tile, dk), BF16),
                        pltpu.VMEM((2, tile, dk), BF16),
                        pltpu.VMEM((2, tile, dv), BF16),
                        pltpu.VMEM((2, tile, dv), BF16),
                        pltpu.VMEM((2, tile, 2 * SMALL_W), F32),
                        pltpu.VMEM((SSD_STATE, d), F32),
                        pltpu.VMEM((TAIL, cdim), BF16),
                        pltpu.VMEM((n_chunks + 1, hk, dv), F32),
                        pltpu.VMEM((CHUNK, hk), F32),
                        pltpu.VMEM((CHUNK, hk), F32),
                        pltpu.VMEM((CHUNK, CHUNK), BF16),
                        pltpu.VMEM((CHUNK, CHUNK), F32),
                        pltpu.VMEM((CHUNK, CHUNK), F32)],
        compiler_params=pltpu.CompilerParams(
            dimension_semantics=("arbitrary",), vmem_limit_bytes=VMEM_LIMIT),
        name="mixer",
    )(h2d, h2d, p["norm_w"], p["w_main"],
      p["conv_w"], p["conv_b"], p["dt_bias"], p["a_log"], p["d_skip"], p["ssd_norm_w"], s0, t0,
      p["gate_w2"], p["gate_b"], p["gla_norm_w"], g0)


def _merge_body(ys_ref, yg_ref, mg_ref, h_ref, wb_ref, wo_ref, fnw_ref, o_ref, *, final_norm):
    d = h_ref.shape[-1]
    bp0 = _dot(ys_ref[...], wb_ref[0])
    bp1 = _dot(yg_ref[...], wb_ref[1])
    g0 = _sigmoid(mg_ref[:, :d].astype(F32))
    g1 = _sigmoid(mg_ref[:, d:].astype(F32))
    merged = (g0 * bp0 + g1 * bp1).astype(BF16)
    out = h_ref[...] + _dot(merged, wo_ref[...])
    if final_norm:
        out = _rmsnorm(out, fnw_ref[...])
    o_ref[...] = out


def _merge(ys, yg, mg, h2d, w_branch, w_out, final_norm_w, tm, final_norm):
    m, d = h2d.shape
    row = lambda i: (i, 0)
    return pl.pallas_call(
        functools.partial(_merge_body, final_norm=final_norm),
        out_shape=jax.ShapeDtypeStruct((m, d), F32),
        grid=(m // tm,),
        in_specs=[
            pl.BlockSpec((tm, d), row),
            pl.BlockSpec((tm, d), row),
            pl.BlockSpec((tm, 2 * d), row),
            pl.BlockSpec((tm, d), row),
            pl.BlockSpec(w_branch.shape, lambda i: (0, 0, 0)),
            pl.BlockSpec(w_out.shape, lambda i: (0, 0)),
            pl.BlockSpec((1, d), lambda i: (0, 0)),
        ],
        out_specs=pl.BlockSpec((tm, d), row),
        compiler_params=pltpu.CompilerParams(
            dimension_semantics=("arbitrary",), vmem_limit_bytes=VMEM_LIMIT),
        name="merge",
    )(ys, yg, mg, h2d, w_branch, w_out, final_norm_w)


def _pad_lanes(v, width):
    v = v.reshape(1, -1)
    return jnp.pad(v, ((0, 0), (0, width - v.shape[1])))


def _layer_params(i, norm_w, w_in, conv_w, conv_b, dt_bias, a_log, d_skip, ssd_norm_w,
                  gla_gate_w2, gla_gate_b, gla_norm_w, w_branch, w_out):
    d = w_out.shape[-1]
    heads = dt_bias.shape[-1]
    rank = gla_gate_w2.shape[1]
    dk = gla_gate_w2.shape[2]
    dv = d
    cdim = conv_w.shape[-1]
    widths = (d, cdim, heads, dk, dk, dv, dv, rank, 2 * d)
    starts = [0]
    for w in widths:
        starts.append(starts[-1] + w)
    col = lambda j: w_in[i][:, starts[j]:starts[j + 1]]
    slab = lambda j: w_in[i][:, starts[j]:starts[j] + SMALL_W]
    assert max(heads, rank) <= SMALL_W and starts[7] + SMALL_W <= w_in.shape[-1]
    w_main = jnp.concatenate(
        [slab(2), slab(7), col(0), col(1), col(3), col(4), col(5), col(6), col(8)], axis=1).astype(BF16)
    dtb, alog = dt_bias[i], a_log[i]
    w2 = jnp.pad(gla_gate_w2[i], ((0, SMALL_W - rank), (0, 0)))
    return dict(
        norm_w=norm_w[i].reshape(1, d), w_main=w_main,
        conv_w=conv_w[i], conv_b=conv_b[i].reshape(1, cdim),
        dt_bias=_pad_lanes(dtb, SMALL_W), a_log=_pad_lanes(alog, SMALL_W),
        d_skip=jnp.repeat(d_skip[i], SSD_HEAD_DIM).reshape(1, d),
        ssd_norm_w=ssd_norm_w[i].reshape(1, d),
        gate_w2=w2, gate_b=gla_gate_b[i].reshape(1, dk), gla_norm_w=gla_norm_w[i].reshape(1, -1),
        w_branch=w_branch[i].astype(BF16), w_out=w_out[i].astype(BF16))


def _layer(h, p, states, tile, tm, n_valid, final_norm_w, final_norm, emit_state):
    b, l, d = h.shape
    outs = _mixer(h, p, states, tile, n_valid, emit_state)
    y_ssd, y_gla, mg = outs[:3]
    out = _merge(y_ssd, y_gla, mg, h.reshape(b * l, d), p["w_branch"], p["w_out"], final_norm_w, tm, final_norm)
    final = (outs[3][0], outs[4][0], outs[5][0]) if emit_state else None
    return out.reshape(b, l, d), final


def kernel(x, meta_tokens, norm_w, w_in, conv_w, conv_b, dt_bias, a_log, d_skip, ssd_norm_w, gla_gate_w2,
           gla_gate_b, gla_norm_w, w_branch, w_out, final_norm_w):
    b, seq, d = x.shape
    depth = norm_w.shape[0]
    n_meta = meta_tokens.shape[0]
    assert TAIL <= n_meta <= CHUNK and n_meta % TAIL == 0 and seq % CHUNK == 0
    tile = MIXER_TILE if seq % MIXER_TILE == 0 else CHUNK
    tm = MERGE_TILE if (b * seq) % MERGE_TILE == 0 else CHUNK
    cdim = conv_w.shape[-1]
    dk = gla_gate_w2.shape[2]
    fnw = final_norm_w.reshape(1, d)
    zero_states = (jnp.zeros((SSD_STATE, d), F32), jnp.zeros((TAIL, cdim), BF16),
                   jnp.zeros((dk // GLA_HEADS, d), F32))
    h_meta = jnp.pad(meta_tokens.astype(x.dtype), ((0, CHUNK - n_meta), (0, 0)))[None]
    h = x
    for i in range(depth):
        p = _layer_params(i, norm_w, w_in, conv_w, conv_b, dt_bias, a_log, d_skip, ssd_norm_w,
                          gla_gate_w2, gla_gate_b, gla_norm_w, w_branch, w_out)
        h_meta, seeded = _layer(h_meta, p, zero_states, CHUNK, CHUNK, n_meta, fnw, False, True)
        h, _ = _layer(h, p, seeded, tile, tm, tile, fnw, i == depth - 1, False)
    return h
```

```python
import functools

import jax
import jax.numpy as jnp
from jax import lax
from jax.experimental import pallas as pl
from jax.experimental.pallas import tpu as pltpu

F32 = jnp.float32
BF16 = jnp.bfloat16

SSD_HEAD_DIM = 64
SSD_GROUPS = 4
SSD_STATE = 128
GLA_HEADS = 4
GLA_GATE_TEMP = 16.0
EPS = 1e-6

CHUNK = 128
SMALL_W = 128
SUBLANES = 8
TAIL = 16
PROJ_LANES = 512
NEG_BIG = -1e30
GLA_SAFE_LOG_DECAY = -60.0
FILL_STRIDE = 2
MIXER_TILE = 4 * CHUNK
MERGE_TILE = 8 * CHUNK
VMEM_LIMIT = 62 * 1024 * 1024


def _sigmoid(x):
    return 0.5 * (jnp.tanh(0.5 * x) + 1.0)


def _silu(x):
    h = 0.5 * x
    return h + h * jnp.tanh(h)


def _softplus(x):
    return jnp.maximum(x, 0.0) + jnp.log(1.0 + jnp.exp(-jnp.abs(x)))


def _rmsnorm(x, w):
    return x * lax.rsqrt(jnp.mean(x * x, axis=-1, keepdims=True) + EPS) * w


def _split2(x):
    hi = x.astype(BF16)
    mid = (x - hi.astype(F32)).astype(BF16)
    return hi, mid


def _dot(a, b):
    return jnp.dot(a, b, preferred_element_type=F32)


def _dot_nt(a, b):
    return lax.dot_general(a, b, (((1,), (1,)), ((), ())), preferred_element_type=F32)


def _dot_tn(a, b):
    return lax.dot_general(a, b, (((0,), (0,)), ((), ())), preferred_element_type=F32)


def _cumsum_rows(tri_bf, x):
    hi, mid = _split2(x)
    return _dot(tri_bf, hi) + _dot(tri_bf, mid)


def _round_robin(generators, fillers, n_fill, stride):
    generators = list(generators)
    n_fill = min(n_fill, len(fillers))
    rnd = 0
    while generators or n_fill:
        for gen in list(generators):
            if next(gen, StopIteration) is StopIteration:
                generators.remove(gen)
        if n_fill and (rnd % stride == 0 or not generators):
            fillers.pop(0)()
            n_fill -= 1
        rnd += 1


def _ssd_decays(sm, dtb_ref, alog_ref, tri_ref, out, *, n_chunks, heads, n_valid):
    head_lane = lax.broadcasted_iota(jnp.int32, (CHUNK, SMALL_W), 1) < heads
    a_full = -jnp.exp(alog_ref[...])
    dts = []
    for c in range(n_chunks):
        dt = _softplus(sm[c * CHUNK:(c + 1) * CHUNK, :] + dtb_ref[...])
        dt = jnp.where(head_lane, dt, 0.0)
        if n_valid is not None:
            row = lax.broadcasted_iota(jnp.int32, (CHUNK, SMALL_W), 0) + c * CHUNK
            dt = jnp.where(row < n_valid, dt, 0.0)
        dts.append(dt)
    yield
    cum_wide = _cumsum_rows(tri_ref[...], jnp.concatenate([dt * a_full for dt in dts], axis=1))
    yield

    for c, dt in enumerate(dts):
        cum = cum_wide[:, c * SMALL_W:(c + 1) * SMALL_W]
        cum_last = cum[CHUNK - 1:CHUNK, :]
        out.append((cum, cum.T, dt.T, jnp.exp(cum), jnp.exp(cum_last - cum) * dt))
    yield


def _ssd_chunk(xbc, z, decay, prm, negmask_ref, y, rows, state, tail):
    cw_ref, cb_ref, dsk_ref, nw_ref = prm
    d_ssd = z.shape[-1]
    cdim = xbc.shape[-1]
    heads = d_ssd // SSD_HEAD_DIM
    gw = d_ssd // SSD_GROUPS
    hpg = heads // SSD_GROUPS
    gs = SSD_STATE
    n_taps = cw_ref.shape[0]

    def conv_silu(lanes):
        cur = xbc[:, lanes].astype(F32)
        hist = tail[:, lanes].astype(F32)[TAIL - SUBLANES:, :]
        ext = jnp.concatenate([hist, cur], axis=0)
        xc = cb_ref[:, lanes] + cw_ref[n_taps - 1:n_taps, lanes] * cur
        for j in range(1, n_taps):
            xc = xc + cw_ref[n_taps - 1 - j:n_taps - j, lanes] * pltpu.roll(ext, j, axis=0)[SUBLANES:, :]
        return _silu(xc)

    bm = conv_silu(slice(d_ssd, d_ssd + SSD_GROUPS * gs)).astype(BF16)
    yield
    cm = conv_silu(slice(d_ssd + SSD_GROUPS * gs, cdim)).astype(BF16)
    yield
    cum, cum_t, dt_t, e_split, w_split = decay

    blk_lane = lax.broadcasted_iota(jnp.int32, (CHUNK, gw), 1) // SSD_HEAD_DIM
    for g in range(SSD_GROUPS):
        lanes = slice(g * gw, (g + 1) * gw)
        bg = bm[:, g * gs:(g + 1) * gs]
        cg = cm[:, g * gs:(g + 1) * gs]
        xs = conv_silu(lanes)
        xs_bf = xs.astype(BF16)
        def spread(v):
            wide = jnp.broadcast_to(v[:, g * hpg:g * hpg + 1], (CHUNK, gw))
            for j in range(1, hpg):
                wide = jnp.where(blk_lane == j, jnp.broadcast_to(v[:, g * hpg + j:g * hpg + j + 1], (CHUNK, gw)), wide)
            return wide

        e_g = spread(e_split)
        xw = (xs * spread(w_split)).astype(BF16)
        cb_g = _dot_nt(cg, bg)
        yield
        m_parts, x_parts = [], []
        for j in range(hpg):
            h = g * hpg + j
            seg = (cum[:, h:h + 1] - cum_t[h:h + 1, :]) + negmask_ref[...]
            m_parts.append((cb_g * jnp.exp(seg) * dt_t[h:h + 1, :]).astype(BF16))
            x_parts.append(jnp.where(blk_lane == j, xs_bf, jnp.zeros_like(xs_bf)))
            if j % 2 == 1:
                yield
        y_diag = _dot(jnp.concatenate(m_parts, axis=1), jnp.concatenate(x_parts, axis=0))
        s_g = state[:, lanes]
        y_off = _dot(cg, s_g.astype(BF16))
        state[:, lanes] = s_g * e_g[CHUNK - 1:CHUNK, :] + _dot_tn(bg, xw)
        yg = y_diag + y_off * e_g + dsk_ref[:, lanes] * xs
        yg = yg * _silu(z[:, lanes].astype(F32))
        y[rows, lanes] = _rmsnorm(yg, nw_ref[:, lanes]).astype(y.dtype)
        yield

    tail[...] = xbc[CHUNK - TAIL:CHUNK, :]


def _gla_decays(sm, w2_ref, gb_ref, tri_ref, out, *, n_chunks, n_valid):
    dk = w2_ref.shape[-1]
    logit = _dot(sm[...].astype(BF16), w2_ref[...].astype(BF16)) + gb_ref[...]
    yield
    log_a = -_softplus(-logit) * (1.0 / GLA_GATE_TEMP)
    if n_valid is not None:
        row = lax.broadcasted_iota(jnp.int32, log_a.shape, 0)
        log_a = jnp.where(row < n_valid, log_a, 0.0)
    yield
    g_wide = _cumsum_rows(
        tri_ref[...], jnp.concatenate([log_a[c * CHUNK:(c + 1) * CHUNK, :] for c in range(n_chunks)], axis=1))
    out.extend(g_wide[:, c * dk:(c + 1) * dk] for c in range(n_chunks))
    yield


class _GlaChunk:
    def __init__(self, q, k, v, gg, g, nw_ref, tri01_ref, y, rows, row0, s_old, s_new, kf_scr, g_scr, *, n_valid):
        self.q, self.k, self.v, self.gg, self.g = q, k, v, gg, g
        self.nw_ref, self.tri01_ref = nw_ref, tri01_ref
        self.y, self.rows, self.row0 = y, rows, row0
        self.s_old, self.s_new = s_old, s_new
        self.kf_scr, self.g_scr = kf_scr, g_scr
        self.n_valid = n_valid
        self.hk = q.shape[-1] // GLA_HEADS
        self.hv = v.shape[-1] // GLA_HEADS
        self.g_mins = []

    def _lanes(self, h):
        return slice(h * self.hk, (h + 1) * self.hk), slice(h * self.hv, (h + 1) * self.hv)

    def _gate_decay(self, ks):
        kf = self.k[:, ks].astype(F32)
        if self.n_valid is not None:
            row = lax.broadcasted_iota(jnp.int32, (CHUNK, self.hk), 0) + self.row0
            kf = jnp.where(row < self.n_valid, kf, 0.0)
        qs = self.q[:, ks].astype(F32) * (self.hk ** -0.5)
        return qs, kf, self.g[:, ks]

    def _emit(self, h, q_dec, att):
        ks, vs = self._lanes(h)
        att = (att * self.tri01_ref[...]).astype(BF16)
        o = _dot(jnp.concatenate([q_dec, att], axis=1),
                 jnp.concatenate([self.s_old[:, vs].astype(BF16), self.v[:, vs]], axis=0))
        gate = _silu(self.gg[:, vs].astype(F32))
        self.y[self.rows, vs] = (_rmsnorm(o, self.nw_ref[...]) * gate).astype(self.y.dtype)

    def stages(self):
        for h in range(GLA_HEADS):
            ks, vs = self._lanes(h)
            qs, kf, g = self._gate_decay(ks)
            yield
            g_last = g[CHUNK - 1:CHUNK, :]
            q_dec = (qs * jnp.exp(g)).astype(BF16)
            k_inv = (kf * jnp.exp(jnp.minimum(-g, -GLA_SAFE_LOG_DECAY))).astype(BF16)
            k_dec = (kf * jnp.exp(g_last - g)).astype(BF16)
            att = _dot_nt(q_dec, k_inv)
            self.g_mins.append(jnp.min(g_last))
            yield
            self._emit(h, q_dec, att)
            decay_col = jnp.broadcast_to(jnp.exp(g_last), (CHUNK, self.hk)).T
            decay_col = jnp.concatenate([decay_col] * (self.hv // self.hk), axis=1)
            self.s_new[:, vs] = self.s_old[:, vs] * decay_col + _dot_tn(k_dec, self.v[:, vs])
            yield

    def redo_unsafe(self):
        g_min = functools.reduce(jnp.minimum, self.g_mins)

        @pl.when(g_min < GLA_SAFE_LOG_DECAY)
        def _():
            col_id = lax.broadcasted_iota(jnp.int32, (CHUNK, CHUNK), 1)
            for h in range(GLA_HEADS):
                ks, _ = self._lanes(h)
                q_h, kf, g_h = self._gate_decay(ks)
                self.kf_scr[...] = kf
                self.g_scr[...] = g_h

                def key_rows(sb, att, q_h=q_h, g_h=g_h):
                    s0 = pl.multiple_of(sb * SUBLANES, SUBLANES)
                    k_rows = self.kf_scr[pl.ds(s0, SUBLANES), :]
                    g_rows = self.g_scr[pl.ds(s0, SUBLANES), :]
                    for j in range(SUBLANES):
                        p = q_h * k_rows[j:j + 1, :] * jnp.exp(jnp.minimum(g_h - g_rows[j:j + 1, :], 0.0))
                        att = jnp.where(col_id == s0 + j, jnp.sum(p, axis=1, keepdims=True), att)
                    return att

                att = lax.fori_loop(0, CHUNK // SUBLANES, key_rows, jnp.zeros((CHUNK, CHUNK), F32))
                self._emit(h, (q_h * jnp.exp(g_h)).astype(BF16), att)


def _mixer_body(h_ref, hn_ref, nw_ref, wm_ref,
                cw_ref, cb_ref, dtb_ref, alog_ref, dsk_ref, snw_ref, s0_ref, t0_ref,
                w2_ref, gb_ref, gnw_ref, g0_ref,
                ys_ref, yg_ref, sf_ref, tf_ref, gf_ref,
                u_s, z_s, xbc_s, q_s, k_s, v_s, gg_s, sm_s, ssd_state, tail, gla_state, kf_scr, g_scr,
                tri_s, tri01_s, negmask_s,
                *, tile, n_valid, tiles_per_seq):
    i = pl.program_id(0)
    n_chunks = tile // CHUNK
    d = h_ref.shape[-1]
    heads = d // SSD_HEAD_DIM
    cur = i % 2
    nxt = 1 - cur

    slots = (("z", z_s), ("xbc", xbc_s), ("q", q_s), ("k", k_s), ("v", v_s), ("gg", gg_s))
    off = {}
    start = 2 * SMALL_W
    for name, ref in slots:
        off[name] = start
        start += ref.shape[-1]

    def normalise_jobs(src_ref, slot):
        def job(r0):
            def run():
                u_s[slot, r0:r0 + CHUNK, :] = _rmsnorm(src_ref[r0:r0 + CHUNK, :], nw_ref[...]).astype(BF16)
            return run
        return [job(r0) for r0 in range(0, tile, CHUNK)]

    def project_jobs(slot):
        def small():
            sm_s[slot] = _dot(u_s[slot], wm_ref[:, :2 * SMALL_W])
        jobs = [small]
        for name in ("xbc", "z", "q", "k", "v", "gg"):
            ref = dict(slots)[name]
            width = ref.shape[-1]
            step = min(width, PROJ_LANES)
            for c0 in range(0, width, step):
                def job(ref=ref, c0=c0, step=step, o=off[name]):
                    ref[slot, :, c0:c0 + step] = _dot(u_s[slot], wm_ref[:, o + c0:o + c0 + step]).astype(ref.dtype)
                jobs.append(job)
        return jobs

    @pl.when(i == 0)
    def _():
        tr = lax.broadcasted_iota(jnp.int32, (CHUNK, CHUNK), 0)
        tc = lax.broadcasted_iota(jnp.int32, (CHUNK, CHUNK), 1)
        tri_s[...] = (tr >= tc).astype(BF16)
        tri01_s[...] = (tr >= tc).astype(F32)
        negmask_s[...] = jnp.where(tr >= tc, 0.0, NEG_BIG)
        for job in normalise_jobs(h_ref, 0) + project_jobs(0):
            job()

    @pl.when(i % tiles_per_seq == 0)
    def _():
        ssd_state[...] = s0_ref[...]
        tail[...] = t0_ref[...]
        gla_state[0] = g0_ref[...]

    ssd_prm = (cw_ref, cb_ref, dsk_ref, snw_ref)
    nv = None if n_valid == tile else n_valid

    fillers = normalise_jobs(hn_ref, nxt) + project_jobs(nxt)
    ssd_decays, gla_decays = [], []
    _round_robin(
        [_ssd_decays(sm_s.at[cur, :, :SMALL_W], dtb_ref, alog_ref, tri_s, ssd_decays,
                     n_chunks=n_chunks, heads=heads, n_valid=nv),
         _gla_decays(sm_s.at[cur, :, SMALL_W:], w2_ref, gb_ref, tri_s, gla_decays, n_chunks=n_chunks, n_valid=nv)],
        fillers, 0, 1)
    per_chunk = -(-len(fillers) // n_chunks)
    glas = []
    for c in range(n_chunks):
        r0 = c * CHUNK
        rows = slice(r0, r0 + CHUNK)
        view = lambda ref: ref.at[cur, rows]
        gla = _GlaChunk(view(q_s), view(k_s), view(v_s), view(gg_s), gla_decays[c], gnw_ref, tri01_s,
                        yg_ref, rows, r0, gla_state.at[c], gla_state.at[c + 1], kf_scr, g_scr, n_valid=nv)
        ssd = _ssd_chunk(view(xbc_s), view(z_s), ssd_decays[c], ssd_prm, negmask_s, ys_ref, rows, ssd_state, tail)
        _round_robin([ssd, gla.stages()], fillers, per_chunk, FILL_STRIDE)
        glas.append(gla)
    assert not fillers

    for gla in glas:
        gla.redo_unsafe()
    gla_state[0] = gla_state[n_chunks]

    if sf_ref is not None:
        @pl.when(i % tiles_per_seq == tiles_per_seq - 1)
        def _():
            sf_ref[0] = ssd_state[...]
            tf_ref[0] = xbc_s[cur, n_valid - TAIL:n_valid, :]
            gf_ref[0] = gla_state[n_chunks]


def _mixer_body_no_state(*refs, **kw):
    n_in, n_out = 16, 2
    ins, outs, scratch = refs[:n_in], refs[n_in:n_in + n_out], refs[n_in + n_out:]
    _mixer_body(*ins, *outs, None, None, None, *scratch, **kw)


def _mixer(h, p, states, tile, n_valid, emit_state):
    b, l, d = h.shape
    cdim = p["conv_w"].shape[-1]
    assert p["conv_w"].shape[0] - 1 <= SUBLANES
    dk = p["gate_b"].shape[-1]
    dv = d
    hk = dk // GLA_HEADS
    n_main = p["w_main"].shape[1]
    tiles_per_seq = l // tile
    n_tiles = b * tiles_per_seq
    n_chunks = tile // CHUNK
    s0, t0, g0 = states
    h2d = h.reshape(b * l, d)
    tok = lambda i: (i, 0)
    nxt = lambda i: (jnp.minimum(i + 1, n_tiles - 1), 0)
    const = lambda i: (0, 0)
    per_b = lambda i: (i // tiles_per_seq, 0, 0)
    whole = lambda shape: pl.BlockSpec(shape, const)
    resident = lambda shape: pl.BlockSpec(shape, const, pipeline_mode=pl.Buffered(1))
    out_shape = [jax.ShapeDtypeStruct((b * l, d), BF16), jax.ShapeDtypeStruct((b * l, dv), BF16)]
    out_specs = [pl.BlockSpec((tile, d), tok), pl.BlockSpec((tile, dv), tok)]
    if emit_state:
        out_shape += [jax.ShapeDtypeStruct((b, SSD_STATE, d), F32),
                      jax.ShapeDtypeStruct((b, TAIL, cdim), BF16),
                      jax.ShapeDtypeStruct((b, hk, dv), F32)]
        out_specs += [pl.BlockSpec((1, SSD_STATE, d), per_b),
                      pl.BlockSpec((1, TAIL, cdim), per_b),
                      pl.BlockSpec((1, hk, dv), per_b)]
    kw = dict(tile=tile, n_valid=n_valid, tiles_per_seq=tiles_per_seq)
    body = functools.partial(_mixer_body if emit_state else _mixer_body_no_state, **kw)
    return pl.pallas_call(
        body,
        out_shape=out_shape,
        grid=(n_tiles,),
        in_specs=[
            resident((tile, d)),
            pl.BlockSpec((tile, d), nxt),
            whole((1, d)),
            resident((d, n_main)),
            whole(p["conv_w"].shape), whole((1, cdim)), whole((1, SMALL_W)), whole((1, SMALL_W)),
            whole((1, d)), whole((1, d)), whole((SSD_STATE, d)), whole((TAIL, cdim)),
            whole((SMALL_W, dk)), whole((1, dk)), whole((1, dv // GLA_HEADS)), whole((hk, dv)),
        ],
        out_specs=out_specs,
        scratch_shapes=[pltpu.VMEM((2, tile, d), BF16),
                        pltpu.VMEM((2, tile, d), BF16),
                        pltpu.VMEM((2, tile, cdim), BF16),
                        pltpu.VMEM((2, tile, dk), BF16),
                        pltpu.VMEM((2, tile, dk), BF16),
                        pltpu.VMEM((2, tile, dv), BF16),
                        pltpu.VMEM((2, tile, dv), BF16),
                        pltpu.VMEM((2, tile, 2 * SMALL_W), F32),
                        pltpu.VMEM((SSD_STATE, d), F32),
                        pltpu.VMEM((TAIL, cdim), BF16),
                        pltpu.VMEM((n_chunks + 1, hk, dv), F32),
                        pltpu.VMEM((CHUNK, hk), F32),
                        pltpu.VMEM((CHUNK, hk), F32),
                        pltpu.VMEM((CHUNK, CHUNK), BF16),
                        pltpu.VMEM((CHUNK, CHUNK), F32),
                        pltpu.VMEM((CHUNK, CHUNK), F32)],
        compiler_params=pltpu.CompilerParams(
            dimension_semantics=("arbitrary",), vmem_limit_bytes=VMEM_LIMIT),
        name="mixer",
    )(h2d, h2d, p["norm_w"], p["w_main"],
      p["conv_w"], p["conv_b"], p["dt_bias"], p["a_log"], p["d_skip"], p["ssd_norm_w"], s0, t0,
      p["gate_w2"], p["gate_b"], p["gla_norm_w"], g0)


def _merge_body(ys_ref, yg_ref, h_ref, nw_ref, wg_ref, wb_ref, wo_ref, fnw_ref, o_ref, *, final_norm):
    d = h_ref.shape[-1]
    h = h_ref[...]
    u = _rmsnorm(h, nw_ref[...]).astype(BF16)
    bp0 = _dot(ys_ref[...], wb_ref[0])
    bp1 = _dot(yg_ref[...], wb_ref[1])
    g0 = _sigmoid(_dot(u, wg_ref[:, :d]))
    g1 = _sigmoid(_dot(u, wg_ref[:, d:]))
    merged = (g0 * bp0 + g1 * bp1).astype(BF16)
    out = h + _dot(merged, wo_ref[...])
    if final_norm:
        out = _rmsnorm(out, fnw_ref[...])
    o_ref[...] = out


def _merge(ys, yg, h2d, norm_w, w_gate, w_branch, w_out, final_norm_w, tm, final_norm):
    m, d = h2d.shape
    row = lambda i: (i, 0)
    return pl.pallas_call(
        functools.partial(_merge_body, final_norm=final_norm),
        out_shape=jax.ShapeDtypeStruct((m, d), F32),
        grid=(m // tm,),
        in_specs=[
            pl.BlockSpec((tm, d), row),
            pl.BlockSpec((tm, d), row),
            pl.BlockSpec((tm, d), row),
            pl.BlockSpec((1, d), lambda i: (0, 0)),
            pl.BlockSpec(w_gate.shape, lambda i: (0, 0)),
            pl.BlockSpec(w_branch.shape, lambda i: (0, 0, 0)),
            pl.BlockSpec(w_out.shape, lambda i: (0, 0)),
            pl.BlockSpec((1, d), lambda i: (0, 0)),
        ],
        out_specs=pl.BlockSpec((tm, d), row),
        compiler_params=pltpu.CompilerParams(
            dimension_semantics=("arbitrary",), vmem_limit_bytes=VMEM_LIMIT),
        name="merge",
    )(ys, yg, h2d, norm_w, w_gate, w_branch, w_out, final_norm_w)


def _pad_lanes(v, width):
    v = v.reshape(1, -1)
    return jnp.pad(v, ((0, 0), (0, width - v.shape[1])))


def _layer_params(i, norm_w, w_in, conv_w, conv_b, dt_bias, a_log, d_skip, ssd_norm_w,
                  gla_gate_w2, gla_gate_b, gla_norm_w, w_branch, w_out):
    d = w_out.shape[-1]
    heads = dt_bias.shape[-1]
    rank = gla_gate_w2.shape[1]
    dk = gla_gate_w2.shape[2]
    dv = d
    cdim = conv_w.shape[-1]
    widths = (d, cdim, heads, dk, dk, dv, dv, rank, 2 * d)
    starts = [0]
    for w in widths:
        starts.append(starts[-1] + w)
    col = lambda j: w_in[i][:, starts[j]:starts[j + 1]]
    slab = lambda j: w_in[i][:, starts[j]:starts[j] + SMALL_W]
    assert max(heads, rank) <= SMALL_W and starts[7] + SMALL_W <= w_in.shape[-1]
    w_main = jnp.concatenate(
        [slab(2), slab(7), col(0), col(1), col(3), col(4), col(5), col(6)], axis=1).astype(BF16)
    dtb, alog = dt_bias[i], a_log[i]
    w2 = jnp.pad(gla_gate_w2[i], ((0, SMALL_W - rank), (0, 0)))
    return dict(
        norm_w=norm_w[i].reshape(1, d), w_main=w_main, w_gate=col(8).astype(BF16),
        conv_w=conv_w[i], conv_b=conv_b[i].reshape(1, cdim),
        dt_bias=_pad_lanes(dtb, SMALL_W), a_log=_pad_lanes(alog, SMALL_W),
        d_skip=jnp.repeat(d_skip[i], SSD_HEAD_DIM).reshape(1, d),
        ssd_norm_w=ssd_norm_w[i].reshape(1, d),
        gate_w2=w2, gate_b=gla_gate_b[i].reshape(1, dk), gla_norm_w=gla_norm_w[i].reshape(1, -1),
        w_branch=w_branch[i].astype(BF16), w_out=w_out[i].astype(BF16))


def _layer(h, p, states, tile, tm, n_valid, final_norm_w, final_norm, emit_state):
    b, l, d = h.shape
    outs = _mixer(h, p, states, tile, n_valid, emit_state)
    y_ssd, y_gla = outs[:2]
    out = _merge(y_ssd, y_gla, h.reshape(b * l, d), p["norm_w"], p["w_gate"], p["w_branch"], p["w_out"],
                 final_norm_w, tm, final_norm)
    final = (outs[2][0], outs[3][0], outs[4][0]) if emit_state else None
    return out.reshape(b, l, d), final


def kernel(x, meta_tokens, norm_w, w_in, conv_w, conv_b, dt_bias, a_log, d_skip, ssd_norm_w, gla_gate_w2,
           gla_gate_b, gla_norm_w, w_branch, w_out, final_norm_w):
    b, seq, d = x.shape
    depth = norm_w.shape[0]
    n_meta = meta_tokens.shape[0]
    assert TAIL <= n_meta <= CHUNK and n_meta % TAIL == 0 and seq % CHUNK == 0
    tile = MIXER_TILE if seq % MIXER_TILE == 0 else CHUNK
    tm = MERGE_TILE if (b * seq) % MERGE_TILE == 0 else CHUNK
    cdim = conv_w.shape[-1]
    dk = gla_gate_w2.shape[2]
    fnw = final_norm_w.reshape(1, d)
    zero_states = (jnp.zeros((SSD_STATE, d), F32), jnp.zeros((TAIL, cdim), BF16),
                   jnp.zeros((dk // GLA_HEADS, d), F32))
    h_meta = jnp.pad(meta_tokens.astype(x.dtype), ((0, CHUNK - n_meta), (0, 0)))[None]
    h = x
    for i in range(depth):
        p = _layer_params(i, norm_w, w_in, conv_w, conv_b, dt_bias, a_log, d_skip, ssd_norm_w,
                          gla_gate_w2, gla_gate_b, gla_norm_w, w_branch, w_out)
        h_meta, seeded = _layer(h_meta, p, zero_states, CHUNK, CHUNK, n_meta, fnw, False, True)
        h, _ = _layer(h, p, seeded, tile, tm, tile, fnw, i == depth - 1, False)
    return h
```
